```python
import math
import jax
import jax.numpy as jnp
from jax import lax
import numpy as np

D_MODEL = 1024
BATCH = 8
SEQ = 8192
DEPTH = 1
DEC_BATCH = 128
DEC_SEQ = 8
PAST_LEN = 8192
PAGE_SIZE = 128

M_HEADS = 4
M_DK = 128
M_DV = 128
M_QK_WIDTH = M_HEADS * M_DK
M_WIDTH = M_HEADS * M_DV
MLSTM_CHUNK = 64
A_GROUPS = ((128, 1), (512, 4), (2048, 16))
A_HPG = 4
A_HD = 64
A_HEADS = len(A_GROUPS) * A_HPG
A_WIDTH = A_HEADS * A_HD
A_OUT_WIDTH = A_HPG * A_HD
A_BLOCK = 128
ROPE_THETA = 10000.0
D_FF = 2816
EPS = 1e-6
COL_SIZES = (M_QK_WIDTH, M_QK_WIDTH, M_WIDTH, M_HEADS, M_HEADS, M_WIDTH,
             A_WIDTH, A_WIDTH, A_WIDTH, D_MODEL, D_MODEL)
D_IN = sum(COL_SIZES)

kernel_name = "mlstm_dilated_swa_gated_macaron_step"

F32 = jnp.float32


def rmsnorm(x, g):
    xf = x.astype(F32)
    y = xf * lax.rsqrt(jnp.mean(xf * xf, axis=-1, keepdims=True) + EPS)
    return (y * g.astype(F32)).astype(x.dtype)


def swiglu(h, w_gate, w_up, w_down):
    return (jax.nn.silu(h @ w_gate) * (h @ w_up)) @ w_down


def rope(t, pos):
    dh = t.shape[-1]
    half = dh // 2
    inv_freq = ROPE_THETA ** (-2.0 * jnp.arange(half, dtype=F32) / dh)
    ang = pos.astype(F32)[:, None] * inv_freq[None, :]
    cos = jnp.cos(ang)[None, :, None, :]
    sin = jnp.sin(ang)[None, :, None, :]
    tf = t.astype(F32)
    t1, t2 = tf[..., :half], tf[..., half:]
    return jnp.concatenate([t1 * cos - t2 * sin, t1 * sin + t2 * cos], axis=-1).astype(t.dtype)


def mlstm_chunk_step(carry, inp):
    C, n, m = carry
    q, k, v, ig, lf = inp
    L = q.shape[2]
    b = jnp.cumsum(lf, axis=-1)
    causal = jnp.tril(jnp.ones((L, L), dtype=bool))
    log_w = jnp.where(causal, b[..., :, None] - b[..., None, :] + ig[..., None, :], -jnp.inf)
    log_inter = b + m[..., None]
    m_t = jnp.maximum(log_inter, jnp.max(log_w, axis=-1))
    w_intra = jnp.exp(log_w - m_t[..., None])
    w_inter = jnp.exp(log_inter - m_t)
    qk = jnp.einsum('bhtk,bhsk->bhts', q, k) * w_intra
    num = w_inter[..., None] * jnp.einsum('bhvk,bhtk->bhtv', C, q) + jnp.einsum('bhts,bhsv->bhtv', qk, v)
    den = w_inter * jnp.einsum('bhk,bhtk->bht', n, q) + jnp.sum(qk, axis=-1)
    h = num / jnp.maximum(jnp.abs(den), jnp.exp(-m_t))[..., None]
    m_new = m_t[..., -1]
    decay = jnp.exp(b[..., -1] + m - m_new)
    w_s = jnp.exp(b[..., -1:] - b + ig - m_new[..., None])
    C_new = decay[..., None, None] * C + jnp.einsum('bhs,bhsv,bhsk->bhvk', w_s, v, k)
    n_new = decay[..., None] * n + jnp.einsum('bhs,bhsk->bhk', w_s, k)
    return (C_new, n_new, m_new), h


def mlstm_scan(q, k, v, ig, lf, C, n, m):
    B, S, H, _ = q.shape
    L = math.gcd(S, MLSTM_CHUNK)
    nc = S // L

    def to_chunks(t):
        return t.astype(F32).reshape(B, nc, L, H, -1).transpose(1, 0, 3, 2, 4)

    def gate_chunks(t):
        return t.astype(F32).reshape(B, nc, L, H).transpose(1, 0, 3, 2)

    (C, n, m), h = lax.scan(
        mlstm_chunk_step,
        (C.astype(F32), n.astype(F32), m.astype(F32)),
        (to_chunks(q), to_chunks(k), to_chunks(v), gate_chunks(ig), gate_chunks(lf)))
    h = h.transpose(1, 0, 3, 2, 4).reshape(B, S, H, -1)
    return h, (C, n, m)


def dilated_attn_prompt(q, k, v, window, dilation):
    B, S, H, dh = q.shape
    n_sub = S // dilation
    reach = window // dilation
    nb = -(-n_sub // A_BLOCK)
    n_pad = nb * A_BLOCK
    Z = B * dilation

    def residues(t):
        return t.reshape(B, n_sub, dilation, H, dh).transpose(0, 2, 1, 3, 4).reshape(Z, n_sub, H, dh)

    qb = jnp.pad(residues(q), ((0, 0), (0, n_pad - n_sub), (0, 0), (0, 0))).reshape(Z, nb, A_BLOCK, H, dh)

    def key_blocks(t):
        tp = jnp.pad(residues(t), ((0, 0), (A_BLOCK, n_pad - n_sub), (0, 0), (0, 0)))
        tp = tp.reshape(Z, nb + 1, A_BLOCK, H, dh)
        return jnp.concatenate([tp[:, :-1], tp[:, 1:]], axis=2)

    kb = key_blocks(k)
    vb = key_blocks(v)
    qi = jnp.arange(nb)[:, None] * A_BLOCK + jnp.arange(A_BLOCK)[None, :]
    ki = jnp.arange(nb)[:, None] * A_BLOCK + jnp.arange(2 * A_BLOCK)[None, :] - A_BLOCK
    dist = qi[:, :, None] - ki[:, None, :]
    valid = (dist >= 0) & (dist <= reach) & (ki[:, None, :] >= 0)
    s = jnp.einsum('znqhd,znkhd->znhqk', qb.astype(F32), kb.astype(F32)) * (dh ** -0.5)
    s = jnp.where(valid[None, :, None], s, -jnp.inf)
    mx = jnp.max(s, axis=-1, keepdims=True)
    e = jnp.exp(s - mx)
    den = jnp.sum(e, axis=-1, keepdims=True)
    o = jnp.einsum('znhqk,znkhd->znqhd', e / den, vb.astype(F32))
    lse = (mx + jnp.log(den))[..., 0]
    o = o.reshape(B, dilation, n_pad, H, dh)[:, :, :n_sub].transpose(0, 2, 1, 3, 4).reshape(B, S, H, dh)
    lse = lse.transpose(0, 1, 3, 2).reshape(B, dilation, n_pad, H)[:, :, :n_sub]
    lse = lse.transpose(0, 2, 1, 3).reshape(B, S, H)
    return o, lse


def dilated_attn_cached(q, k, v, kv_cache, window, dilation):
    B, T, H, dh = q.shape
    n_cached = kv_cache.shape[1]
    k_all = jnp.concatenate([kv_cache[:, :, 0].astype(k.dtype), k], axis=1)
    v_all = jnp.concatenate([kv_cache[:, :, 1].astype(v.dtype), v], axis=1)
    reach = window // dilation
    idx = n_cached + jnp.arange(T)[:, None] - dilation * jnp.arange(reach + 1)[None, :]
    valid = idx >= 0
    idx = jnp.maximum(idx, 0)
    kg = k_all[:, idx]
    vg = v_all[:, idx]
    s = jnp.einsum('bthd,btjhd->bhtj', q.astype(F32), kg.astype(F32)) * (dh ** -0.5)
    s = jnp.where(valid[None, None], s, -jnp.inf)
    mx = jnp.max(s, axis=-1, keepdims=True)
    e = jnp.exp(s - mx)
    den = jnp.sum(e, axis=-1, keepdims=True)
    o = jnp.einsum('bhtj,btjhd->bthd', e / den, vg.astype(F32))
    lse = (mx + jnp.log(den))[..., 0].transpose(0, 2, 1)
    return o, lse


def hybrid_layer(x, pos, C0, n0, m0, win_kv, g_ffn1, w1_gate, w1_up, w1_down, g_mix, w_in,
                 b_igate, b_fgate, g_mhead, w_up_m, w_up_a, w_out, g_ffn2, w2_gate, w2_up, w2_down):
    B, S, _ = x.shape
    dt = x.dtype
    x = x + 0.5 * swiglu(rmsnorm(x, g_ffn1), w1_gate, w1_up, w1_down)
    h = rmsnorm(x, g_mix)
    z = h @ w_in
    offs = np.cumsum(COL_SIZES)[:-1].tolist()
    qm, km, vm, ig, fg, og, qa, ka, va, gm, ga = jnp.split(z, offs, axis=-1)

    qm = qm.reshape(B, S, M_HEADS, M_DK)
    km = km.reshape(B, S, M_HEADS, M_DK) * (M_DK ** -0.5)
    vm = vm.reshape(B, S, M_HEADS, M_DV)
    ig_log = ig.astype(F32) + b_igate.astype(F32)
    lf = jax.nn.log_sigmoid(fg.astype(F32) + b_fgate.astype(F32))
    hm, (C, n, m) = mlstm_scan(qm, km, vm, ig_log, lf, C0, n0, m0)
    hm = hm * lax.rsqrt(jnp.mean(hm * hm, axis=-1, keepdims=True) + EPS)
    hm = hm.reshape(B, S, M_WIDTH) * g_mhead.astype(F32) * jax.nn.sigmoid(og.astype(F32))
    y_m = hm.astype(dt) @ w_up_m

    qa = rope(qa.reshape(B, S, A_HEADS, A_HD), pos)
    ka = rope(ka.reshape(B, S, A_HEADS, A_HD), pos)
    va = va.reshape(B, S, A_HEADS, A_HD)
    outs, lses, new_kv = [], [], []
    for g, (window, dilation) in enumerate(A_GROUPS):
        sl = slice(g * A_HPG, (g + 1) * A_HPG)
        qg, kg, vg = qa[:, :, sl], ka[:, :, sl], va[:, :, sl]
        kv_rows = jnp.stack([kg, vg], axis=2)
        if win_kv is None:
            o, lse = dilated_attn_prompt(qg, kg, vg, window, dilation)
            new_kv.append(kv_rows[:, S - min(window, S):])
        else:
            o, lse = dilated_attn_cached(qg, kg, vg, win_kv[g], window, dilation)
            new_kv.append(kv_rows)
        outs.append(o)
        lses.append(lse)
    wts = jax.nn.softmax(jnp.stack(lses), axis=0)
    y_a = jnp.einsum('gbsh,gbshd->bshd', wts, jnp.stack(outs)).reshape(B, S, A_OUT_WIDTH).astype(dt)
    y_a = y_a @ w_up_a

    merged = jax.nn.sigmoid(gm) * y_m + jax.nn.sigmoid(ga) * y_a
    x = x + merged @ w_out
    x = x + 0.5 * swiglu(rmsnorm(x, g_ffn2), w2_gate, w2_up, w2_down)
    return x, (C.astype(dt), n.astype(dt), m.astype(dt)), new_kv


def setup_inputs(seed: int = 0) -> dict:
    key = jax.random.key(seed)
    ks = jax.random.split(key, 32)

    def nrm(k, shape, scale):
        return jax.random.normal(k, shape, F32) * scale

    def gain(k, shape):
        return 1.0 + 0.02 * jax.random.normal(k, shape, F32)

    win_shape = lambda w: (DEPTH, DEC_BATCH, min(w, PAST_LEN), 2, A_HPG, A_HD)
    return {
        "x_prompt": nrm(ks[0], (BATCH, SEQ, D_MODEL), 1.0),
        "x_sample": nrm(ks[1], (DEC_BATCH, DEC_SEQ, D_MODEL), 1.0),
        "state_mlstm_C": nrm(ks[2], (DEPTH, DEC_BATCH, M_HEADS, M_DV, M_DK), 0.5),
        "state_mlstm_n": nrm(ks[3], (DEPTH, DEC_BATCH, M_HEADS, M_DK), 0.5),
        "state_mlstm_m": nrm(ks[4], (DEPTH, DEC_BATCH, M_HEADS), 1.0),
        "cache_win128_kv": nrm(ks[5], win_shape(A_GROUPS[0][0]), 1.0),
        "cache_win512_kv": nrm(ks[6], win_shape(A_GROUPS[1][0]), 1.0),
        "cache_win2048_kv": nrm(ks[7], win_shape(A_GROUPS[2][0]), 1.0),
        "g_ffn1": gain(ks[8], (DEPTH, D_MODEL)),
        "w1_gate": nrm(ks[9], (DEPTH, D_MODEL, D_FF), D_MODEL ** -0.5),
        "w1_up": nrm(ks[10], (DEPTH, D_MODEL, D_FF), D_MODEL ** -0.5),
        "w1_down": nrm(ks[11], (DEPTH, D_FF, D_MODEL), D_FF ** -0.5),
        "g_mix": gain(ks[12], (DEPTH, D_MODEL)),
        "w_in": nrm(ks[13], (DEPTH, D_MODEL, D_IN), D_MODEL ** -0.5),
        "b_igate": nrm(ks[14], (DEPTH, M_HEADS), 0.1),
        "b_fgate": 3.0 + 3.0 * jax.random.uniform(ks[15], (DEPTH, M_HEADS), F32),
        "g_mhead": gain(ks[16], (DEPTH, M_WIDTH)),
        "w_up_m": nrm(ks[17], (DEPTH, M_WIDTH, D_MODEL), M_WIDTH ** -0.5),
        "w_up_a": nrm(ks[18], (DEPTH, A_OUT_WIDTH, D_MODEL), A_OUT_WIDTH ** -0.5),
        "w_out": nrm(ks[19], (DEPTH, D_MODEL, D_MODEL), D_MODEL ** -0.5),
        "g_ffn2": gain(ks[20], (DEPTH, D_MODEL)),
        "w2_gate": nrm(ks[21], (DEPTH, D_MODEL, D_FF), D_MODEL ** -0.5),
        "w2_up": nrm(ks[22], (DEPTH, D_MODEL, D_FF), D_MODEL ** -0.5),
        "w2_down": nrm(ks[23], (DEPTH, D_FF, D_MODEL), D_FF ** -0.5),
        "g_final": gain(ks[24], (D_MODEL,)),
    }


def reference(x_prompt, x_sample, state_mlstm_C, state_mlstm_n, state_mlstm_m,
              cache_win128_kv, cache_win512_kv, cache_win2048_kv,
              g_ffn1, w1_gate, w1_up, w1_down, g_mix, w_in, b_igate, b_fgate, g_mhead,
              w_up_m, w_up_a, w_out, g_ffn2, w2_gate, w2_up, w2_down, g_final):
    B, S, _ = x_prompt.shape
    T = x_sample.shape[1]
    pos_prompt = jnp.arange(S, dtype=jnp.int32)
    pos_sample = PAST_LEN + jnp.arange(T, dtype=jnp.int32)
    xp, xs = x_prompt, x_sample
    Cp, Np_, Mp, K1p, K2p, K3p = [], [], [], [], [], []
    Cs, Ns, Ms, K1s, K2s, K3s = [], [], [], [], [], []
    for l in range(DEPTH):
        lw = (g_ffn1[l], w1_gate[l], w1_up[l], w1_down[l], g_mix[l], w_in[l], b_igate[l], b_fgate[l],
              g_mhead[l], w_up_m[l], w_up_a[l], w_out[l], g_ffn2[l], w2_gate[l], w2_up[l], w2_down[l])
        C0 = jnp.zeros((B, M_HEADS, M_DV, M_DK), F32)
        n0 = jnp.zeros((B, M_HEADS, M_DK), F32)
        m0 = jnp.zeros((B, M_HEADS), F32)
        xp, (c, n, m), kv = hybrid_layer(xp, pos_prompt, C0, n0, m0, None, *lw)
        Cp.append(c); Np_.append(n); Mp.append(m)
        K1p.append(kv[0]); K2p.append(kv[1]); K3p.append(kv[2])
        xs, (c, n, m), kv = hybrid_layer(
            xs, pos_sample, state_mlstm_C[l], state_mlstm_n[l], state_mlstm_m[l],
            (cache_win128_kv[l], cache_win512_kv[l], cache_win2048_kv[l]), *lw)
        Cs.append(c); Ns.append(n); Ms.append(m)
        K1s.append(kv[0]); K2s.append(kv[1]); K3s.append(kv[2])
    y_prompt = rmsnorm(xp, g_final)
    y_sample = rmsnorm(xs, g_final)
    return (y_prompt, y_sample,
            jnp.stack(Cp), jnp.stack(Np_), jnp.stack(Mp),
            jnp.stack(K1p), jnp.stack(K2p), jnp.stack(K3p),
            jnp.stack(Cs), jnp.stack(Ns), jnp.stack(Ms),
            jnp.stack(K1s), jnp.stack(K2s), jnp.stack(K3s))
```

```python
import functools

import jax
import jax.numpy as jnp
from jax import lax
from jax.experimental import pallas as pl
from jax.experimental.pallas import tpu as pltpu

F32 = jnp.float32
BF16 = jnp.bfloat16

D_MODEL = 1024
D_FF = 2816
PAST_LEN = 8192
M_HEADS = 4
M_DK = 128
M_DV = 128
M_WIDTH = M_HEADS * M_DV
A_GROUPS = ((128, 1), (512, 4), (2048, 16))
A_HPG = 4
A_HD = 64
A_GW = A_HPG * A_HD
A_WIDTH = len(A_GROUPS) * A_GW
A_BLOCK = 128
ROPE_THETA = 10000.0
EPS = 1e-6
NEG_INF = float("-inf")

_C_QM, _C_KM, _C_VM, _C_IG, _C_FG, _C_OG = 0, 512, 1024, 1536, 1540, 1544
_C_QA, _C_KA, _C_VA, _C_GM, _C_GA, _C_END = 2056, 2824, 3592, 4360, 5384, 6408

VMEM_LIMIT = 56 * 1024 * 1024
NT_DIMS = (((1,), (1,)), ((), ()))
TN_DIMS = (((0,), (0,)), ((), ()))


def _params(n_axes):
    return pltpu.CompilerParams(dimension_semantics=("arbitrary",) * n_axes,
                                vmem_limit_bytes=VMEM_LIMIT)


def _resident(shape):
    return pl.BlockSpec(shape, lambda *_: (0,) * len(shape), pipeline_mode=pl.Buffered(1))


def _rms(x, g):
    return x * lax.rsqrt(jnp.mean(x * x, axis=-1, keepdims=True) + EPS) * g


def _dot(a, b):
    return jnp.dot(a, b, preferred_element_type=F32)


FF_CHUNK = 256


def _ffn_kernel(x_ref, g_ref, wg_ref, wu_ref, wd_ref, *rest, final_norm):
    if final_norm:
        gf_ref, o_ref, act_ref = rest
    else:
        o_ref, act_ref = rest
    x = x_ref[...]
    h = _rms(x, g_ref[...]).astype(BF16)
    for c in range(D_FF // FF_CHUNK):
        sl = slice(c * FF_CHUNK, (c + 1) * FF_CHUNK)
        gate = _dot(h, wg_ref[:, sl])
        up = _dot(h, wu_ref[:, sl])
        act_ref[:, sl] = (gate * jax.nn.sigmoid(gate) * up).astype(BF16)
    out = x + 0.5 * _dot(act_ref[...], wd_ref[...])
    if final_norm:
        out = _rms(out, gf_ref[...])
    o_ref[...] = out


def _ffn(x, g, wg, wu, wd, g_final=None, tm=1024):
    n = x.shape[0]
    tm = min(tm, n)
    final_norm = g_final is not None
    row = pl.BlockSpec((tm, D_MODEL), lambda i: (i, 0))
    in_specs = [row, _resident((1, D_MODEL)), _resident((D_MODEL, D_FF)),
                _resident((D_MODEL, D_FF)), _resident((D_FF, D_MODEL))]
    args = [x, g, wg, wu, wd]
    if final_norm:
        in_specs.append(_resident((1, D_MODEL)))
        args.append(g_final)
    return pl.pallas_call(
        functools.partial(_ffn_kernel, final_norm=final_norm),
        grid=(n // tm,),
        in_specs=in_specs,
        out_specs=row,
        out_shape=jax.ShapeDtypeStruct((n, D_MODEL), F32),
        scratch_shapes=[pltpu.VMEM((tm, D_FF), BF16)],
        compiler_params=_params(1),
        name="ffn_final" if final_norm else "ffn",
    )(*args)


def _rope(t, cos, sin):
    lane = lax.broadcasted_iota(jnp.int32, cos.shape, 1)
    first_half = (lane % A_HD) < (A_HD // 2)
    outs = []
    for c in range(t.shape[1] // 128):
        tc = t[:, c * 128:(c + 1) * 128]
        partner = jnp.where(first_half, pltpu.roll(tc, 128 - A_HD // 2, 1), pltpu.roll(tc, A_HD // 2, 1))
        outs.append(tc * cos + partner * sin)
    return jnp.concatenate(outs, axis=1)


def _inproj_kernel(x_ref, g_ref, w_ref, wgt_ref, cos_ref, sin_ref,
                   qm_ref, km_ref, vm_ref, qa_ref, ka_ref, va_ref, kvf_ref, gt_ref):
    h = _rms(x_ref[...], g_ref[...]).astype(BF16)
    qm_ref[...] = _dot(h, w_ref[:, 0:512]).astype(BF16)
    km_ref[...] = (_dot(h, w_ref[:, 512:1024]) * (M_DK ** -0.5)).astype(BF16)
    vm_ref[...] = _dot(h, w_ref[:, 1024:1536]).astype(BF16)
    cos = cos_ref[...]
    sin = sin_ref[...]
    qa_ref[...] = _rope(_dot(h, w_ref[:, 1536:2304]), cos, sin).astype(BF16)
    ka = _rope(_dot(h, w_ref[:, 2304:3072]), cos, sin)
    ka_ref[...] = ka.astype(BF16)
    va = _dot(h, w_ref[:, 3072:3840])
    va_ref[...] = va.astype(BF16)
    kvf_ref[0, :, 0:A_WIDTH] = ka
    kvf_ref[0, :, A_WIDTH:2 * A_WIDTH] = va
    gt_ref[...] = lax.dot_general(wgt_ref[...], h, NT_DIMS, preferred_element_type=F32)


def _inproj(x, g, w_qkv, w_gate_t, cos, sin, nbatch, seq, tail, tm=512):
    n = x.shape[0]
    tm = min(tm, tail)
    tpb = seq // tm
    first_tail = tpb - tail // tm
    row = lambda w: pl.BlockSpec((tm, w), lambda i: (i, 0))
    table = pl.BlockSpec((tm, 128), lambda i: (i % tpb, 0))
    out_shapes = (
        jax.ShapeDtypeStruct((n, 512), BF16), jax.ShapeDtypeStruct((n, 512), BF16),
        jax.ShapeDtypeStruct((n, 512), BF16),
        jax.ShapeDtypeStruct((n, A_WIDTH), BF16), jax.ShapeDtypeStruct((n, A_WIDTH), BF16),
        jax.ShapeDtypeStruct((n, A_WIDTH), BF16),
        jax.ShapeDtypeStruct((nbatch, tail, 2 * A_WIDTH), F32),
        jax.ShapeDtypeStruct((8, n), F32),
    )
    out_specs = (
        row(512), row(512), row(512), row(A_WIDTH), row(A_WIDTH), row(A_WIDTH),
        pl.BlockSpec((1, tm, 2 * A_WIDTH), lambda i: (i // tpb, jnp.maximum(i % tpb - first_tail, 0), 0)),
        pl.BlockSpec((8, tm), lambda i: (0, i)),
    )
    return pl.pallas_call(
        _inproj_kernel,
        grid=(n // tm,),
        in_specs=[row(D_MODEL), _resident((1, D_MODEL)), _resident(w_qkv.shape),
                  _resident(w_gate_t.shape), table, table],
        out_specs=out_specs,
        out_shape=out_shapes,
        compiler_params=_params(1),
        name="inproj",
    )(x, g, w_qkv, w_gate_t, cos, sin)


def _scan_lanes(x, lane, op, identity):
    shift = 1
    while shift < x.shape[1]:
        x = op(x, jnp.where(lane >= shift, pltpu.roll(x, shift, 1), identity))
        shift *= 2
    return x


def _mlstm_kernel(q_ref, k_ref, v_ref, gt_ref, bias_ref, c0_ref, n0_ref, m0_ref,
                  h_ref, c_ref, n_ref, m_ref, *, L):
    @pl.when(pl.program_id(1) == 0)
    def _():
        c_ref[...] = c0_ref[...]
        n_ref[...] = n0_ref[...]
        m_ref[...] = m0_ref[...]

    gates = gt_ref[...] + bias_ref[...]
    ig = gates[0:4, :]
    fx = gates[4:8, :]
    lf = jnp.minimum(fx, 0.0) - jnp.log1p(jnp.exp(-jnp.abs(fx)))
    lane = lax.broadcasted_iota(jnp.int32, (M_HEADS, L), 1)
    b = _scan_lanes(lf, lane, jnp.add, 0.0)
    a = ig - b
    m_prev = m_ref[0]
    big_m = jnp.maximum(m_prev, _scan_lanes(a, lane, jnp.maximum, NEG_INF))
    m_t = b + big_m
    w_inter = jnp.exp(m_prev - big_m)
    e_negm = jnp.exp(-m_t)
    b_last = b[:, L - 1:L]
    m_new = m_t[:, L - 1:L]
    decay = jnp.exp(b_last + m_prev - m_new)
    w_s = jnp.exp(a + (b_last - m_new))
    cols = jnp.transpose(jnp.concatenate(
        [big_m, w_inter, e_negm, w_s, jnp.zeros((128 - 4 * M_HEADS, L), F32)], axis=0))

    r_idx = lax.broadcasted_iota(jnp.int32, (L, L), 0)
    c_idx = lax.broadcasted_iota(jnp.int32, (L, L), 1)
    causal = c_idx <= r_idx
    for h in range(M_HEADS):
        sl = slice(h * M_DK, (h + 1) * M_DK)
        q = q_ref[0, :, sl]
        k = k_ref[0, :, sl]
        v = v_ref[0, :, sl]
        big_m_col = cols[:, h:h + 1]
        w_inter_col = cols[:, 4 + h:5 + h]
        e_negm_col = cols[:, 8 + h:9 + h]
        w_s_col = cols[:, 12 + h:13 + h]
        s = lax.dot_general(q, k, NT_DIMS, preferred_element_type=F32)
        p = s * jnp.exp(jnp.where(causal, a[h:h + 1, :] - big_m_col, NEG_INF))
        c_old = c_ref[0, h]
        n_old = n_ref[0, h:h + 1, :]
        inter = lax.dot_general(q, c_old.astype(BF16), NT_DIMS, preferred_element_type=F32)
        num = w_inter_col * inter + _dot(p.astype(BF16), v)
        qn = jnp.sum(q.astype(F32) * n_old, axis=-1, keepdims=True)
        den = w_inter_col * qn + jnp.sum(p, axis=-1, keepdims=True)
        hh = num / jnp.maximum(jnp.abs(den), e_negm_col)
        h_ref[0, :, sl] = hh * lax.rsqrt(jnp.mean(hh * hh, axis=-1, keepdims=True) + EPS)
        dec = decay[h:h + 1, :]
        vw = (v.astype(F32) * w_s_col).astype(BF16)
        c_ref[0, h] = dec * c_old + lax.dot_general(vw, k, TN_DIMS, preferred_element_type=F32)
        w_rows = jnp.broadcast_to(w_s[h:h + 1, :], (8, L)).astype(BF16)
        n_ref[0, h:h + 1, :] = dec * n_old + _dot(w_rows, k)[0:1, :]
    m_ref[0] = m_new


def _mlstm(q, k, v, gates_t, bias, c0, n0, m0, L):
    nb, seq, _ = q.shape
    nchunk = seq // L
    tok = pl.BlockSpec((1, L, M_WIDTH), lambda b, c: (b, c, 0))
    st_c = pl.BlockSpec((1, M_HEADS, M_DV, M_DK), lambda b, c: (b, 0, 0, 0))
    st_n = pl.BlockSpec((1, M_HEADS, M_DK), lambda b, c: (b, 0, 0))
    st_m = pl.BlockSpec((1, M_HEADS, 1), lambda b, c: (b, 0, 0))
    return pl.pallas_call(
        functools.partial(_mlstm_kernel, L=L),
        grid=(nb, nchunk),
        in_specs=[tok, tok, tok, pl.BlockSpec((8, L), lambda b, c: (0, b * nchunk + c)),
                  pl.BlockSpec((8, 1), lambda b, c: (0, 0)), st_c, st_n, st_m],
        out_specs=(tok, st_c, st_n, st_m),
        out_shape=(jax.ShapeDtypeStruct((nb, seq, M_WIDTH), F32),
                   jax.ShapeDtypeStruct(c0.shape, F32), jax.ShapeDtypeStruct(n0.shape, F32),
                   jax.ShapeDtypeStruct(m0.shape, F32)),
        compiler_params=_params(2),
        name="mlstm",
    )(q, k, v, gates_t, bias, c0, n0, m0)


def _attn_kernel(q_ref, kp_ref, kc_ref, vp_ref, vc_ref, o_ref, l_ref):
    has_prev = pl.program_id(2) > 0
    r_idx = lax.broadcasted_iota(jnp.int32, (A_BLOCK, A_BLOCK), 0)
    c_idx = lax.broadcasted_iota(jnp.int32, (A_BLOCK, A_BLOCK), 1)
    cur_ok = c_idx <= r_idx
    prev_ok = jnp.logical_and(c_idx >= r_idx, has_prev)
    for h in range(A_HPG):
        sl = slice(h * A_HD, (h + 1) * A_HD)
        q = q_ref[0, :, sl]
        s_c = lax.dot_general(q, kc_ref[0, :, sl], NT_DIMS, preferred_element_type=F32) * (A_HD ** -0.5)
        s_p = lax.dot_general(q, kp_ref[0, :, sl], NT_DIMS, preferred_element_type=F32) * (A_HD ** -0.5)
        s_c = jnp.where(cur_ok, s_c, NEG_INF)
        s_p = jnp.where(prev_ok, s_p, NEG_INF)
        mx = jnp.maximum(jnp.max(s_c, axis=-1, keepdims=True), jnp.max(s_p, axis=-1, keepdims=True))
        e_c = jnp.exp(s_c - mx)
        e_p = jnp.exp(s_p - mx)
        den = jnp.sum(e_c, axis=-1, keepdims=True) + jnp.sum(e_p, axis=-1, keepdims=True)
        o = _dot(e_c.astype(BF16), vc_ref[0, :, sl]) + _dot(e_p.astype(BF16), vp_ref[0, :, sl])
        o_ref[0, :, sl] = o / den
        l_ref[0, :, sl] = jnp.broadcast_to(mx + jnp.log(den), (A_BLOCK, A_HD))


def _attn_prompt(qa, ka, va, nbatch, seq, group):
    _, dil = A_GROUPS[group]
    n_sub = seq // dil
    nblk = n_sub // A_BLOCK
    ncol = A_WIDTH // A_GW
    view = lambda t: t.reshape(nbatch, n_sub, dil * A_WIDTH)
    blk = (1, A_BLOCK, A_GW)
    cur = pl.BlockSpec(blk, lambda b, r, i: (b, i, ncol * r + group))
    prev = pl.BlockSpec(blk, lambda b, r, i: (b, jnp.maximum(i - 1, 0), ncol * r + group))
    out = pl.BlockSpec(blk, lambda b, r, i: (b, i, r))
    shape = jax.ShapeDtypeStruct((nbatch, n_sub, dil * A_GW), F32)
    o, lse = pl.pallas_call(
        _attn_kernel,
        grid=(nbatch, dil, nblk),
        in_specs=[cur, prev, cur, prev, cur],
        out_specs=(out, out),
        out_shape=(shape, shape),
        compiler_params=_params(3),
        name=f"attn_prompt_d{dil}",
    )(view(qa), view(ka), view(ka), view(va), view(va))
    return o.reshape(nbatch * seq, A_GW), lse.reshape(nbatch * seq, A_GW)


NEW_PAD = 128


def _attn_cached_kernel(q_ref, kvn_ref, c1_ref, c2_ref, c3_ref, o_ref, l_ref, *, T):
    rows = A_HPG * T
    q = q_ref[0]
    kvn = kvn_ref[0]
    row_head = lax.broadcasted_iota(jnp.int32, (rows, A_GW), 0) // T
    lane_head = lax.broadcasted_iota(jnp.int32, (rows, A_GW), 1) // A_HD
    out_head = lax.broadcasted_iota(jnp.int32, (T, A_GW), 1) // A_HD
    pad = jnp.zeros((NEW_PAD - T, A_GW), F32)
    for g, (window, dil) in enumerate(A_GROUPS):
        cref = (c1_ref, c2_ref, c3_ref)[g]
        n_cached = cref.shape[1]
        gs = slice(g * A_GW, (g + 1) * A_GW)
        qg = q[:, gs]
        qbd = jnp.where(row_head == lane_head, jnp.concatenate([qg] * A_HPG, axis=0), 0.0).astype(BF16)
        kc = cref[0, :, 0:A_GW].astype(BF16)
        vc = cref[0, :, A_GW:2 * A_GW].astype(BF16)
        kn = jnp.concatenate([kvn[:, gs], pad], axis=0).astype(BF16)
        vn = jnp.concatenate([kvn[:, A_WIDTH + g * A_GW:A_WIDTH + (g + 1) * A_GW], pad], axis=0).astype(BF16)
        s_c = lax.dot_general(qbd, kc, NT_DIMS, preferred_element_type=F32) * (A_HD ** -0.5)
        s_n = lax.dot_general(qbd, kn, NT_DIMS, preferred_element_type=F32) * (A_HD ** -0.5)
        t_c = lax.broadcasted_iota(jnp.int32, (rows, n_cached), 0) % T
        jd_c = n_cached + t_c - lax.broadcasted_iota(jnp.int32, (rows, n_cached), 1)
        ok_c = jnp.logical_and(jd_c % dil == 0, jd_c <= window)
        t_n = lax.broadcasted_iota(jnp.int32, (rows, NEW_PAD), 0) % T
        r_n = lax.broadcasted_iota(jnp.int32, (rows, NEW_PAD), 1)
        jd_n = t_n - r_n
        ok_n = jnp.logical_and(jnp.logical_and(jd_n >= 0, jd_n % dil == 0), jd_n <= window)
        s_c = jnp.where(ok_c, s_c, NEG_INF)
        s_n = jnp.where(ok_n, s_n, NEG_INF)
        mx = jnp.maximum(jnp.max(s_c, axis=-1, keepdims=True), jnp.max(s_n, axis=-1, keepdims=True))
        e_c = jnp.exp(s_c - mx)
        e_n = jnp.exp(s_n - mx)
        den = jnp.sum(e_c, axis=-1, keepdims=True) + jnp.sum(e_n, axis=-1, keepdims=True)
        res = (_dot(e_c.astype(BF16), vc) + _dot(e_n.astype(BF16), vn)) / den
        lse = mx + jnp.log(den)
        o = jnp.zeros((T, A_GW), F32)
        lb = jnp.zeros((T, A_GW), F32)
        for h in range(A_HPG):
            o = jnp.where(out_head == h, res[h * T:(h + 1) * T, :], o)
            lb = jnp.where(out_head == h, lse[h * T:(h + 1) * T, :], lb)
        o_ref[0, :, gs] = o
        l_ref[0, :, gs] = lb


def _attn_cached(q, kvn, caches):
    nb, T, _ = q.shape
    per_b = lambda t: pl.BlockSpec((1,) + t.shape[1:], lambda b: (b, 0, 0))
    shape = jax.ShapeDtypeStruct((nb, T, A_WIDTH), F32)
    return pl.pallas_call(
        functools.partial(_attn_cached_kernel, T=T),
        grid=(nb,),
        in_specs=[per_b(q), per_b(kvn)] + [per_b(c) for c in caches],
        out_specs=(per_b(q), per_b(q)),
        out_shape=(shape, shape),
        compiler_params=_params(1),
        name="attn_cached",
    )(q, kvn, *caches)


def _post_kernel(x_ref, hm_ref, o1_ref, o2_ref, o3_ref, l1_ref, l2_ref, l3_ref,
                 gmix_ref, gmh_ref, wz_ref, wum_ref, wua_ref, wo_ref, out_ref):
    x = x_ref[...]
    h = _rms(x, gmix_ref[...]).astype(BF16)
    og = _dot(h, wz_ref[:, 0:M_WIDTH])
    gm = _dot(h, wz_ref[:, M_WIDTH:M_WIDTH + D_MODEL])
    ga = _dot(h, wz_ref[:, M_WIDTH + D_MODEL:M_WIDTH + 2 * D_MODEL])
    hm = hm_ref[...] * gmh_ref[...] * jax.nn.sigmoid(og)
    y_m = _dot(hm.astype(BF16), wum_ref[...])
    l1, l2, l3 = l1_ref[...], l2_ref[...], l3_ref[...]
    mx = jnp.maximum(jnp.maximum(l1, l2), l3)
    e1, e2, e3 = jnp.exp(l1 - mx), jnp.exp(l2 - mx), jnp.exp(l3 - mx)
    y_a = (e1 * o1_ref[...] + e2 * o2_ref[...] + e3 * o3_ref[...]) / (e1 + e2 + e3)
    y_a = _dot(y_a.astype(BF16), wua_ref[...])
    merged = jax.nn.sigmoid(gm) * y_m + jax.nn.sigmoid(ga) * y_a
    out_ref[...] = x + _dot(merged.astype(BF16), wo_ref[...])


def _post(x, hm, outs, lses, g_mix, g_mhead, w_z, w_up_m, w_up_a, w_out, tm=512):
    n = x.shape[0]
    tm = min(tm, n)
    row = lambda w: pl.BlockSpec((tm, w), lambda i: (i, 0))
    return pl.pallas_call(
        _post_kernel,
        grid=(n // tm,),
        in_specs=[row(D_MODEL), row(M_WIDTH)] + [row(A_GW)] * 6 +
                 [_resident((1, D_MODEL)), _resident((1, M_WIDTH)), _resident(w_z.shape),
                  _resident(w_up_m.shape), _resident(w_up_a.shape), _resident(w_out.shape)],
        out_specs=row(D_MODEL),
        out_shape=jax.ShapeDtypeStruct((n, D_MODEL), F32),
        compiler_params=_params(1),
        name="post",
    )(x, hm, *outs, *lses, g_mix, g_mhead, w_z, w_up_m, w_up_a, w_out)


def _rope_tables(pos):
    half = A_HD // 2
    inv_freq = ROPE_THETA ** (-2.0 * jnp.arange(half, dtype=F32) / A_HD)
    ang = pos.astype(F32)[:, None] * inv_freq[None, :]
    cos = jnp.cos(ang)
    sin = jnp.sin(ang)
    return jnp.tile(cos, (1, 4)), jnp.tile(jnp.concatenate([-sin, sin], axis=1), (1, 2))


def _window_rows(kvf, nbatch, rows):
    res = []
    for g, r in enumerate(rows):
        kk = kvf[:, kvf.shape[1] - r:, g * A_GW:(g + 1) * A_GW].reshape(nbatch, r, A_HPG, A_HD)
        vv = kvf[:, kvf.shape[1] - r:, A_WIDTH + g * A_GW:A_WIDTH + (g + 1) * A_GW].reshape(nbatch, r, A_HPG, A_HD)
        res.append(jnp.stack([kk, vv], axis=2))
    return res


def kernel(x_prompt, x_sample, state_mlstm_C, state_mlstm_n, state_mlstm_m, cache_win128_kv, cache_win512_kv, cache_win2048_kv, g_ffn1, w1_gate, w1_up, w1_down, g_mix, w_in, b_igate, b_fgate, g_mhead, w_up_m, w_up_a, w_out, g_ffn2, w2_gate, w2_up, w2_down, g_final):
    B, S, _ = x_prompt.shape
    BS, T, _ = x_sample.shape
    depth = w_in.shape[0]
    caches_all = (cache_win128_kv, cache_win512_kv, cache_win2048_kv)

    cos_p, sin_p = _rope_tables(jnp.arange(S, dtype=jnp.int32))
    cos_s, sin_s = _rope_tables(PAST_LEN + jnp.arange(T, dtype=jnp.int32))
    cos_s = jnp.tile(cos_s, (BS, 1))
    sin_s = jnp.tile(sin_s, (BS, 1))

    xp = x_prompt.reshape(B * S, D_MODEL)
    xs = x_sample.reshape(BS * T, D_MODEL)
    L_P = 256
    L_S = 128
    prompt_rows = tuple(min(w, S) for w, _ in A_GROUPS)
    tail = max(prompt_rows)
    acc = [[] for _ in range(12)]
    for l in range(depth):
        bf = lambda w: w[l].astype(BF16)
        vec = lambda g: g[l].reshape(1, -1)
        wi = w_in[l]
        w_qkv = jnp.concatenate([wi[:, _C_QM:_C_IG], wi[:, _C_QA:_C_GM]], axis=1).astype(BF16)
        w_gate_t = wi[:, _C_IG:_C_OG].T.astype(BF16)
        w_z = jnp.concatenate([wi[:, _C_OG:_C_QA], wi[:, _C_GM:_C_END]], axis=1).astype(BF16)
        ffn1 = (vec(g_ffn1), bf(w1_gate), bf(w1_up), bf(w1_down))
        ffn2 = (vec(g_ffn2), bf(w2_gate), bf(w2_up), bf(w2_down))
        post_w = (vec(g_mix), vec(g_mhead), w_z, bf(w_up_m), bf(w_up_a), bf(w_out))
        gate_bias = jnp.concatenate([b_igate[l], b_fgate[l]]).reshape(8, 1)
        last = l == depth - 1
        gfin = g_final.reshape(1, -1) if last else None

        xp = _ffn(xp, *ffn1)
        qm, km, vm, qa, ka, va, kvf, gt = _inproj(xp, vec(g_mix), w_qkv, w_gate_t, cos_p, sin_p, B, S, tail)
        hm, c_p, n_p, m_p = _mlstm(
            qm.reshape(B, S, M_WIDTH), km.reshape(B, S, M_WIDTH), vm.reshape(B, S, M_WIDTH), gt, gate_bias,
            jnp.zeros((B, M_HEADS, M_DV, M_DK), F32), jnp.zeros((B, M_HEADS, M_DK), F32),
            jnp.zeros((B, M_HEADS, 1), F32), L_P)
        o_l = [_attn_prompt(qa, ka, va, B, S, g) for g in range(len(A_GROUPS))]
        xp = _post(xp, hm.reshape(B * S, M_WIDTH), [o for o, _ in o_l], [s for _, s in o_l], *post_w)
        xp = _ffn(xp, *ffn2, g_final=gfin)
        kv_p = _window_rows(kvf, B, prompt_rows)
        for i, t in enumerate((c_p, n_p, m_p.reshape(B, M_HEADS), *kv_p)):
            acc[i].append(t)

        xs = _ffn(xs, *ffn1)
        qm, km, vm, qa, ka, va, kvf, gt = _inproj(xs, vec(g_mix), w_qkv, w_gate_t, cos_s, sin_s, 1, BS * T, BS * T)
        pad_tok = lambda t: jnp.pad(t.reshape(BS, T, M_WIDTH), ((0, 0), (0, L_S - T), (0, 0)))
        gt = gt.reshape(8, BS, T)
        gt = jnp.concatenate([jnp.pad(gt[0:4], ((0, 0), (0, 0), (0, L_S - T)), constant_values=-1e30),
                              jnp.pad(gt[4:8], ((0, 0), (0, 0), (0, L_S - T)), constant_values=1e30)], axis=0)
        hm, c_s, n_s, m_s = _mlstm(
            pad_tok(qm), pad_tok(km), pad_tok(vm), gt.reshape(8, BS * L_S), gate_bias,
            state_mlstm_C[l], state_mlstm_n[l], state_mlstm_m[l].reshape(BS, M_HEADS, 1), L_S)
        hm = hm[:, :T].reshape(BS * T, M_WIDTH)
        caches = [c[l].reshape(BS, c.shape[2], 2 * A_GW) for c in caches_all]
        o_s, l_s = _attn_cached(qa.astype(F32).reshape(BS, T, A_WIDTH), kvf.reshape(BS, T, 2 * A_WIDTH), caches)
        o_s = o_s.reshape(BS * T, A_WIDTH)
        l_s = l_s.reshape(BS * T, A_WIDTH)
        split = lambda t: [t[:, g * A_GW:(g + 1) * A_GW] for g in range(len(A_GROUPS))]
        xs = _post(xs, hm, split(o_s), split(l_s), *post_w)
        xs = _ffn(xs, *ffn2, g_final=gfin)
        kv_s = _window_rows(kvf.reshape(BS, T, 2 * A_WIDTH), BS, (T,) * len(A_GROUPS))
        for i, t in enumerate((c_s, n_s, m_s.reshape(BS, M_HEADS), *kv_s)):
            acc[6 + i].append(t)

    stacked = [jnp.stack(a) for a in acc]
    return (xp.reshape(B, S, D_MODEL), xs.reshape(BS, T, D_MODEL), *stacked)
```

```python
import functools

import jax
import jax.numpy as jnp
from jax import lax
from jax.experimental import pallas as pl
from jax.experimental.pallas import tpu as pltpu

F32 = jnp.float32
BF16 = jnp.bfloat16

D_MODEL = 1024
D_FF = 2816
PAST_LEN = 8192
M_HEADS = 4
M_DK = 128
M_DV = 128
M_WIDTH = M_HEADS * M_DV
A_GROUPS = ((128, 1), (512, 4), (2048, 16))
A_HPG = 4
A_HD = 64
A_GW = A_HPG * A_HD
A_WIDTH = len(A_GROUPS) * A_GW
A_BLOCK = 128
ROPE_THETA = 10000.0
EPS = 1e-6
NEG_INF = float("-inf")

_C_QM, _C_KM, _C_VM, _C_IG, _C_FG, _C_OG = 0, 512, 1024, 1536, 1540, 1544
_C_QA, _C_KA, _C_VA, _C_GM, _C_GA, _C_END = 2056, 2824, 3592, 4360, 5384, 6408

VMEM_LIMIT = 56 * 1024 * 1024
NT_DIMS = (((1,), (1,)), ((), ()))
TN_DIMS = (((0,), (0,)), ((), ()))


def _params(n_axes):
    return pltpu.CompilerParams(dimension_semantics=("arbitrary",) * n_axes,
                                vmem_limit_bytes=VMEM_LIMIT)


def _resident(shape):
    return pl.BlockSpec(shape, lambda *_: (0,) * len(shape), pipeline_mode=pl.Buffered(1))


def _rms(x, g):
    return x * lax.rsqrt(jnp.mean(x * x, axis=-1, keepdims=True) + EPS) * g


def _dot(a, b):
    return jnp.dot(a, b, preferred_element_type=F32)


FF_CHUNK = 256


def _ffn_kernel(x_ref, g_ref, wg_ref, wu_ref, wd_ref, *rest, final_norm):
    if final_norm:
        gf_ref, o_ref, act_ref = rest
    else:
        o_ref, act_ref = rest
    x = x_ref[...]
    h = _rms(x, g_ref[...]).astype(BF16)
    for c in range(D_FF // FF_CHUNK):
        sl = slice(c * FF_CHUNK, (c + 1) * FF_CHUNK)
        gate = _dot(h, wg_ref[:, sl])
        up = _dot(h, wu_ref[:, sl])
        act_ref[:, sl] = (gate * jax.nn.sigmoid(gate) * up).astype(BF16)
    out = x + 0.5 * _dot(act_ref[...], wd_ref[...])
    if final_norm:
        out = _rms(out, gf_ref[...])
    o_ref[...] = out


def _ffn(x, g, wg, wu, wd, g_final=None, tm=1024):
    n = x.shape[0]
    tm = min(tm, n)
    final_norm = g_final is not None
    row = pl.BlockSpec((tm, D_MODEL), lambda i: (i, 0))
    in_specs = [row, _resident((1, D_MODEL)), _resident((D_MODEL, D_FF)),
                _resident((D_MODEL, D_FF)), _resident((D_FF, D_MODEL))]
    args = [x, g, wg, wu, wd]
    if final_norm:
        in_specs.append(_resident((1, D_MODEL)))
        args.append(g_final)
    return pl.pallas_call(
        functools.partial(_ffn_kernel, final_norm=final_norm),
        grid=(n // tm,),
        in_specs=in_specs,
        out_specs=row,
        out_shape=jax.ShapeDtypeStruct((n, D_MODEL), F32),
        scratch_shapes=[pltpu.VMEM((tm, D_FF), BF16)],
        compiler_params=_params(1),
        name="ffn_final" if final_norm else "ffn",
    )(*args)


def _rope(t, cos, sin):
    lane = lax.broadcasted_iota(jnp.int32, cos.shape, 1)
    first_half = (lane % A_HD) < (A_HD // 2)
    outs = []
    for c in range(t.shape[1] // 128):
        tc = t[:, c * 128:(c + 1) * 128]
        partner = jnp.where(first_half, pltpu.roll(tc, 128 - A_HD // 2, 1), pltpu.roll(tc, A_HD // 2, 1))
        outs.append(tc * cos + partner * sin)
    return jnp.concatenate(outs, axis=1)


def _inproj_kernel(x_ref, g_ref, w_ref, wgt_ref, cos_ref, sin_ref,
                   qm_ref, km_ref, vm_ref, q1_ref, q2_ref, q3_ref, k1_ref, k2_ref, k3_ref,
                   v1_ref, v2_ref, v3_ref, kvf_ref, gt_ref, z_ref, *, dils):
    tm = x_ref.shape[0]

    def emit(slot, z, refs):
        for c in range(A_WIDTH // 128):
            z_ref[slot, c] = z[:, c * 128:(c + 1) * 128]
        per_group = A_GW // 128
        for g, (ref, d) in enumerate(zip(refs, dils)):
            for r in range(d):
                for p in range(per_group):
                    ref[0, r, :, p * 128:(p + 1) * 128] = z_ref[
                        slot, g * per_group + p, pl.ds(r, tm // d, stride=d), :].astype(BF16)

    h = _rms(x_ref[...], g_ref[...]).astype(BF16)
    qm_ref[...] = _dot(h, w_ref[:, 0:512]).astype(BF16)
    km_ref[...] = (_dot(h, w_ref[:, 512:1024]) * (M_DK ** -0.5)).astype(BF16)
    vm_ref[...] = _dot(h, w_ref[:, 1024:1536]).astype(BF16)
    cos = cos_ref[...]
    sin = sin_ref[...]
    emit(0, _rope(_dot(h, w_ref[:, 1536:2304]), cos, sin) * (A_HD ** -0.5), (q1_ref, q2_ref, q3_ref))
    ka = _rope(_dot(h, w_ref[:, 2304:3072]), cos, sin)
    emit(1, ka, (k1_ref, k2_ref, k3_ref))
    va = _dot(h, w_ref[:, 3072:3840])
    emit(2, va, (v1_ref, v2_ref, v3_ref))
    kvf_ref[0, :, 0:A_WIDTH] = ka
    kvf_ref[0, :, A_WIDTH:2 * A_WIDTH] = va
    gt_ref[...] = lax.dot_general(wgt_ref[...], h, NT_DIMS, preferred_element_type=F32)


def _inproj(x, g, w_qkv, w_gate_t, cos, sin, nbatch, seq, tail, dils, tm=512):
    n = x.shape[0]
    tm = min(tm, tail)
    tpb = seq // tm
    first_tail = tpb - tail // tm
    row = lambda w: pl.BlockSpec((tm, w), lambda i: (i, 0))
    table = pl.BlockSpec((tm, 128), lambda i: (i % tpb, 0))
    grouped_shape = [jax.ShapeDtypeStruct((nbatch, d, seq // d, A_GW), BF16) for d in dils]
    grouped_spec = [pl.BlockSpec((1, d, tm // d, A_GW), lambda i: (i // tpb, 0, i % tpb, 0)) for d in dils]
    out_shapes = (
        jax.ShapeDtypeStruct((n, 512), BF16), jax.ShapeDtypeStruct((n, 512), BF16),
        jax.ShapeDtypeStruct((n, 512), BF16),
        *grouped_shape, *grouped_shape, *grouped_shape,
        jax.ShapeDtypeStruct((nbatch, tail, 2 * A_WIDTH), F32),
        jax.ShapeDtypeStruct((8, n), F32),
    )
    out_specs = (
        row(512), row(512), row(512), *grouped_spec, *grouped_spec, *grouped_spec,
        pl.BlockSpec((1, tm, 2 * A_WIDTH), lambda i: (i // tpb, jnp.maximum(i % tpb - first_tail, 0), 0)),
        pl.BlockSpec((8, tm), lambda i: (0, i)),
    )
    return pl.pallas_call(
        functools.partial(_inproj_kernel, dils=dils),
        grid=(n // tm,),
        in_specs=[row(D_MODEL), _resident((1, D_MODEL)), _resident(w_qkv.shape),
                  _resident(w_gate_t.shape), table, table],
        out_specs=out_specs,
        out_shape=out_shapes,
        scratch_shapes=[pltpu.VMEM((3, A_WIDTH // 128, tm, 128), F32)],
        compiler_params=_params(1),
        name="inproj",
    )(x, g, w_qkv, w_gate_t, cos, sin)


def _scan_lanes(x, lane, op, identity):
    shift = 1
    while shift < x.shape[1]:
        x = op(x, jnp.where(lane >= shift, pltpu.roll(x, shift, 1), identity))
        shift *= 2
    return x


def _mlstm_kernel(q_ref, k_ref, v_ref, gt_ref, bias_ref, c0_ref, n0_ref, m0_ref,
                  h_ref, c_ref, n_ref, m_ref, *, L):
    @pl.when(pl.program_id(1) == 0)
    def _():
        c_ref[...] = c0_ref[...]
        n_ref[...] = n0_ref[...]
        m_ref[...] = m0_ref[...]

    gates = gt_ref[...] + bias_ref[...]
    ig = gates[0:4, :]
    fx = gates[4:8, :]
    lf = jnp.minimum(fx, 0.0) - jnp.log1p(jnp.exp(-jnp.abs(fx)))
    lane = lax.broadcasted_iota(jnp.int32, (M_HEADS, L), 1)
    b = _scan_lanes(lf, lane, jnp.add, 0.0)
    a = ig - b
    m_prev = m_ref[0]
    big_m = jnp.maximum(m_prev, _scan_lanes(a, lane, jnp.maximum, NEG_INF))
    m_t = b + big_m
    w_inter = jnp.exp(m_prev - big_m)
    e_negm = jnp.exp(-m_t)
    b_last = b[:, L - 1:L]
    m_new = m_t[:, L - 1:L]
    decay = jnp.exp(b_last + m_prev - m_new)
    w_s = jnp.exp(a + (b_last - m_new))
    cols = jnp.transpose(jnp.concatenate(
        [big_m, w_inter, e_negm, w_s, jnp.zeros((128 - 4 * M_HEADS, L), F32)], axis=0))

    r_idx = lax.broadcasted_iota(jnp.int32, (L, L), 0)
    c_idx = lax.broadcasted_iota(jnp.int32, (L, L), 1)
    causal = c_idx <= r_idx
    for h in range(M_HEADS):
        sl = slice(h * M_DK, (h + 1) * M_DK)
        q = q_ref[0, :, sl]
        k = k_ref[0, :, sl]
        v = v_ref[0, :, sl]
        big_m_col = cols[:, h:h + 1]
        w_inter_col = cols[:, 4 + h:5 + h]
        e_negm_col = cols[:, 8 + h:9 + h]
        w_s_col = cols[:, 12 + h:13 + h]
        s = lax.dot_general(q, k, NT_DIMS, preferred_element_type=F32)
        p = s * jnp.exp(jnp.where(causal, a[h:h + 1, :] - big_m_col, NEG_INF))
        c_old = c_ref[0, h]
        n_old = n_ref[0, h:h + 1, :]
        inter = lax.dot_general(q, c_old.astype(BF16), NT_DIMS, preferred_element_type=F32)
        num = w_inter_col * inter + _dot(p.astype(BF16), v)
        qn = jnp.sum(q.astype(F32) * n_old, axis=-1, keepdims=True)
        den = w_inter_col * qn + jnp.sum(p, axis=-1, keepdims=True)
        hh = num / jnp.maximum(jnp.abs(den), e_negm_col)
        h_ref[0, :, sl] = hh * lax.rsqrt(jnp.mean(hh * hh, axis=-1, keepdims=True) + EPS)
        dec = decay[h:h + 1, :]
        vw = (v.astype(F32) * w_s_col).astype(BF16)
        c_ref[0, h] = dec * c_old + lax.dot_general(vw, k, TN_DIMS, preferred_element_type=F32)
        w_rows = jnp.broadcast_to(w_s[h:h + 1, :], (8, L)).astype(BF16)
        n_ref[0, h:h + 1, :] = dec * n_old + _dot(w_rows, k)[0:1, :]
    m_ref[0] = m_new


def _mlstm(q, k, v, gates_t, bias, c0, n0, m0, L):
    nb, seq, _ = q.shape
    nchunk = seq // L
    tok = pl.BlockSpec((1, L, M_WIDTH), lambda b, c: (b, c, 0))
    st_c = pl.BlockSpec((1, M_HEADS, M_DV, M_DK), lambda b, c: (b, 0, 0, 0))
    st_n = pl.BlockSpec((1, M_HEADS, M_DK), lambda b, c: (b, 0, 0))
    st_m = pl.BlockSpec((1, M_HEADS, 1), lambda b, c: (b, 0, 0))
    return pl.pallas_call(
        functools.partial(_mlstm_kernel, L=L),
        grid=(nb, nchunk),
        in_specs=[tok, tok, tok, pl.BlockSpec((8, L), lambda b, c: (0, b * nchunk + c)),
                  pl.BlockSpec((8, 1), lambda b, c: (0, 0)), st_c, st_n, st_m],
        out_specs=(tok, st_c, st_n, st_m),
        out_shape=(jax.ShapeDtypeStruct((nb, seq, M_WIDTH), F32),
                   jax.ShapeDtypeStruct(c0.shape, F32), jax.ShapeDtypeStruct(n0.shape, F32),
                   jax.ShapeDtypeStruct(m0.shape, F32)),
        compiler_params=_params(2),
        name="mlstm",
    )(q, k, v, gates_t, bias, c0, n0, m0)


DIL_MAX = max(d for _, d in A_GROUPS)
TQ = A_BLOCK * DIL_MAX


def _group_weights_combine(o_l):
    mx = functools.reduce(jnp.maximum, [l for _, l in o_l])
    es = [jnp.exp(l - mx) for _, l in o_l]
    num = functools.reduce(jnp.add, [e * o for e, (o, _) in zip(es, o_l)])
    return num / functools.reduce(jnp.add, es)


def _attn_fused_kernel(*refs):
    ng = len(A_GROUPS)
    ins = [refs[5 * g:5 * g + 5] for g in range(ng)]
    y_ref = refs[5 * ng]
    o_scs = refs[5 * ng + 1:5 * ng + 1 + ng]
    l_scs = refs[5 * ng + 1 + ng:]
    not_first = pl.program_id(1) > 0
    q_t = lax.broadcasted_iota(jnp.int32, (2 * A_BLOCK, 2 * A_BLOCK), 0) % A_BLOCK
    key = lax.broadcasted_iota(jnp.int32, (2 * A_BLOCK, 2 * A_BLOCK), 1)
    band = jnp.logical_and(key >= q_t, key <= q_t + A_BLOCK)
    band_block0 = jnp.logical_and(band, jnp.logical_or(key >= A_BLOCK, not_first))
    low = lax.broadcasted_iota(jnp.int32, (A_BLOCK, 2 * A_HD), 1) < A_HD

    def head_pair(qp, ks, vs, mask):
        zero = jnp.zeros_like(qp)
        q2 = jnp.concatenate([jnp.where(low, qp, zero), jnp.where(low, zero, qp)], axis=0)
        s = jnp.where(mask, lax.dot_general(q2, ks, NT_DIMS, preferred_element_type=F32), NEG_INF)
        mx = jnp.max(s, axis=-1, keepdims=True)
        e = jnp.exp(s - mx)
        den = jnp.sum(e, axis=-1, keepdims=True)
        o2 = _dot(e.astype(BF16), vs) / den
        lse = mx + jnp.log(den)
        o = jnp.where(low, o2[0:A_BLOCK], o2[A_BLOCK:])
        lb = jnp.where(low, jnp.broadcast_to(lse[0:A_BLOCK], low.shape), jnp.broadcast_to(lse[A_BLOCK:], low.shape))
        return o, lb

    def unit(q, ks, vs, mask, o_sc, l_sc, r, row0):
        for p in range(A_GW // 128):
            cs = slice(p * 128, (p + 1) * 128)
            o, lb = head_pair(q[:, cs], ks[:, cs], vs[:, cs], mask)
            o_sc[p, r, pl.ds(row0, A_BLOCK), :] = o
            l_sc[p, r, pl.ds(row0, A_BLOCK), :] = lb

    for g, (_, d) in enumerate(A_GROUPS):
        q_ref, k_ref, kp_ref, v_ref, vp_ref = ins[g]
        o_sc, l_sc = o_scs[g], l_scs[g]
        nblk = TQ // d // A_BLOCK
        for r in range(d):
            unit(q_ref[0, r, 0:A_BLOCK, :],
                 jnp.concatenate([kp_ref[0, r], k_ref[0, r, 0:A_BLOCK, :]], axis=0),
                 jnp.concatenate([vp_ref[0, r], v_ref[0, r, 0:A_BLOCK, :]], axis=0),
                 band_block0, o_sc, l_sc, r, 0)
            if nblk > 1:
                def body(i, carry, q_ref=q_ref, k_ref=k_ref, v_ref=v_ref, o_sc=o_sc, l_sc=l_sc, r=r):
                    row0 = pl.multiple_of(i * A_BLOCK, A_BLOCK)
                    prev0 = pl.multiple_of((i - 1) * A_BLOCK, A_BLOCK)
                    unit(q_ref[0, r, pl.ds(row0, A_BLOCK), :], k_ref[0, r, pl.ds(prev0, 2 * A_BLOCK), :],
                         v_ref[0, r, pl.ds(prev0, 2 * A_BLOCK), :], band, o_sc, l_sc, r, row0)
                    return carry
                lax.fori_loop(1, nblk, body, 0)

    for rho in range(DIL_MAX):
        for p in range(A_GW // 128):
            o_l = []
            for g, (_, d) in enumerate(A_GROUPS):
                rows = pl.ds(rho // d, A_BLOCK, stride=DIL_MAX // d) if d < DIL_MAX else slice(None)
                o_l.append((o_scs[g][p, rho % d, rows, :], l_scs[g][p, rho % d, rows, :]))
            y_ref[p, pl.ds(rho, A_BLOCK, stride=DIL_MAX), :] = _group_weights_combine(o_l)


def _attn_prompt(qs, ks, vs, nbatch, seq):
    nt = seq // TQ
    npair = A_GW // 128
    in_specs, args, scratch = [], [], []
    for g, (_, d) in enumerate(A_GROUPS):
        nblk = TQ // d // A_BLOCK
        cur = pl.BlockSpec((1, d, TQ // d, A_GW), lambda b, j: (b, 0, j, 0))
        prev = pl.BlockSpec((1, d, A_BLOCK, A_GW), lambda b, j, nblk=nblk: (b, 0, jnp.maximum(j * nblk - 1, 0), 0))
        in_specs += [cur, cur, prev, cur, prev]
        args += [qs[g], ks[g], ks[g], vs[g], vs[g]]
    for _ in range(2):
        scratch += [pltpu.VMEM((npair, d, TQ // d, 128), F32) for _, d in A_GROUPS]
    return pl.pallas_call(
        _attn_fused_kernel,
        grid=(nbatch, nt),
        in_specs=in_specs,
        out_specs=pl.BlockSpec((npair, TQ, 128), lambda b, j: (0, b * nt + j, 0)),
        out_shape=jax.ShapeDtypeStruct((npair, nbatch * seq, 128), F32),
        scratch_shapes=scratch,
        compiler_params=_params(2),
        name="attn_prompt",
    )(*args)


NEW_PAD = 128


def _attn_cached_kernel(q1_ref, q2_ref, q3_ref, kvn_ref, c1_ref, c2_ref, c3_ref, y_ref, *, T):
    rows = A_HPG * T
    kvn = kvn_ref[0]
    row_head = lax.broadcasted_iota(jnp.int32, (rows, A_GW), 0) // T
    lane_head = lax.broadcasted_iota(jnp.int32, (rows, A_GW), 1) // A_HD
    out_head = lax.broadcasted_iota(jnp.int32, (T, A_GW), 1) // A_HD
    pad = jnp.zeros((NEW_PAD - T, A_GW), F32)
    o_l = []
    for g, (window, dil) in enumerate(A_GROUPS):
        cref = (c1_ref, c2_ref, c3_ref)[g]
        n_cached = cref.shape[2]
        qg = (q1_ref, q2_ref, q3_ref)[g][0]
        qbd = jnp.where(row_head == lane_head, jnp.concatenate([qg] * A_HPG, axis=0), 0.0).astype(BF16)
        k_t = cref[0, 0:A_GW, :].astype(BF16)
        v_t = cref[0, A_GW:2 * A_GW, :].astype(BF16)
        kn = jnp.concatenate([kvn[:, g * A_GW:(g + 1) * A_GW], pad], axis=0).astype(BF16)
        vn = jnp.concatenate([kvn[:, A_WIDTH + g * A_GW:A_WIDTH + (g + 1) * A_GW], pad], axis=0).astype(BF16)
        s_c = _dot(qbd, k_t)
        s_n = lax.dot_general(qbd, kn, NT_DIMS, preferred_element_type=F32)
        t_c = lax.broadcasted_iota(jnp.int32, (rows, n_cached), 0) % T
        jd_c = n_cached + t_c - lax.broadcasted_iota(jnp.int32, (rows, n_cached), 1)
        ok_c = jnp.logical_and(jd_c % dil == 0, jd_c <= window)
        t_n = lax.broadcasted_iota(jnp.int32, (rows, NEW_PAD), 0) % T
        jd_n = t_n - lax.broadcasted_iota(jnp.int32, (rows, NEW_PAD), 1)
        ok_n = jnp.logical_and(jnp.logical_and(jd_n >= 0, jd_n % dil == 0), jd_n <= window)
        s_c = jnp.where(ok_c, s_c, NEG_INF)
        s_n = jnp.where(ok_n, s_n, NEG_INF)
        mx = jnp.maximum(jnp.max(s_c, axis=-1, keepdims=True), jnp.max(s_n, axis=-1, keepdims=True))
        e_c = jnp.exp(s_c - mx)
        e_n = jnp.exp(s_n - mx)
        den = jnp.sum(e_c, axis=-1, keepdims=True) + jnp.sum(e_n, axis=-1, keepdims=True)
        res = (lax.dot_general(e_c.astype(BF16), v_t, NT_DIMS, preferred_element_type=F32)
               + _dot(e_n.astype(BF16), vn)) / den
        lse = mx + jnp.log(den)
        o = jnp.zeros((T, A_GW), F32)
        lb = jnp.zeros((T, A_GW), F32)
        for h in range(A_HPG):
            o = jnp.where(out_head == h, res[h * T:(h + 1) * T, :], o)
            lb = jnp.where(out_head == h, lse[h * T:(h + 1) * T, :], lb)
        o_l.append((o, lb))
    y_ref[0] = _group_weights_combine(o_l)


def _attn_cached(qs, kvn, caches):
    nb, T, _ = kvn.shape
    per_b = lambda t: pl.BlockSpec((1,) + t.shape[1:], lambda b: (b, 0, 0))
    return pl.pallas_call(
        functools.partial(_attn_cached_kernel, T=T),
        grid=(nb,),
        in_specs=[per_b(q) for q in qs] + [per_b(kvn)] + [per_b(c) for c in caches],
        out_specs=pl.BlockSpec((1, T, A_GW), lambda b: (b, 0, 0)),
        out_shape=jax.ShapeDtypeStruct((nb, T, A_GW), F32),
        compiler_params=_params(1),
        name="attn_cached",
    )(*qs, kvn, *caches)


def _post_kernel(x_ref, hm_ref, ya_ref, gmix_ref, gmh_ref, wz_ref, wum_ref, wua_ref, wo_ref, out_ref):
    x = x_ref[...]
    h = _rms(x, gmix_ref[...]).astype(BF16)
    og = _dot(h, wz_ref[:, 0:M_WIDTH])
    gm = _dot(h, wz_ref[:, M_WIDTH:M_WIDTH + D_MODEL])
    ga = _dot(h, wz_ref[:, M_WIDTH + D_MODEL:M_WIDTH + 2 * D_MODEL])
    hm = hm_ref[...] * gmh_ref[...] * jax.nn.sigmoid(og)
    y_m = _dot(hm.astype(BF16), wum_ref[...])
    ya = jnp.concatenate([ya_ref[p] for p in range(ya_ref.shape[0])], axis=1)
    y_a = _dot(ya.astype(BF16), wua_ref[...])
    merged = jax.nn.sigmoid(gm) * y_m + jax.nn.sigmoid(ga) * y_a
    out_ref[...] = x + _dot(merged.astype(BF16), wo_ref[...])


def _post(x, hm, y_a, g_mix, g_mhead, w_z, w_up_m, w_up_a, w_out, tm=512):
    n = x.shape[0]
    tm = min(tm, n)
    row = lambda w: pl.BlockSpec((tm, w), lambda i: (i, 0))
    return pl.pallas_call(
        _post_kernel,
        grid=(n // tm,),
        in_specs=[row(D_MODEL), row(M_WIDTH), pl.BlockSpec((y_a.shape[0], tm, 128), lambda i: (0, i, 0)),
                  _resident((1, D_MODEL)), _resident((1, M_WIDTH)), _resident(w_z.shape),
                  _resident(w_up_m.shape), _resident(w_up_a.shape), _resident(w_out.shape)],
        out_specs=row(D_MODEL),
        out_shape=jax.ShapeDtypeStruct((n, D_MODEL), F32),
        compiler_params=_params(1),
        name="post",
    )(x, hm, y_a, g_mix, g_mhead, w_z, w_up_m, w_up_a, w_out)


def _rope_tables(pos):
    half = A_HD // 2
    inv_freq = ROPE_THETA ** (-2.0 * jnp.arange(half, dtype=F32) / A_HD)
    ang = pos.astype(F32)[:, None] * inv_freq[None, :]
    cos = jnp.cos(ang)
    sin = jnp.sin(ang)
    return jnp.tile(cos, (1, 4)), jnp.tile(jnp.concatenate([-sin, sin], axis=1), (1, 2))


def _window_rows(kvf, nbatch, rows):
    res = []
    for g, r in enumerate(rows):
        kk = kvf[:, kvf.shape[1] - r:, g * A_GW:(g + 1) * A_GW].reshape(nbatch, r, A_HPG, A_HD)
        vv = kvf[:, kvf.shape[1] - r:, A_WIDTH + g * A_GW:A_WIDTH + (g + 1) * A_GW].reshape(nbatch, r, A_HPG, A_HD)
        res.append(jnp.stack([kk, vv], axis=2))
    return res


def kernel(x_prompt, x_sample, state_mlstm_C, state_mlstm_n, state_mlstm_m, cache_win128_kv, cache_win512_kv, cache_win2048_kv, g_ffn1, w1_gate, w1_up, w1_down, g_mix, w_in, b_igate, b_fgate, g_mhead, w_up_m, w_up_a, w_out, g_ffn2, w2_gate, w2_up, w2_down, g_final):
    B, S, _ = x_prompt.shape
    BS, T, _ = x_sample.shape
    depth = w_in.shape[0]
    caches_all = (cache_win128_kv, cache_win512_kv, cache_win2048_kv)

    cos_p, sin_p = _rope_tables(jnp.arange(S, dtype=jnp.int32))
    cos_s, sin_s = _rope_tables(PAST_LEN + jnp.arange(T, dtype=jnp.int32))
    cos_s = jnp.tile(cos_s, (BS, 1))
    sin_s = jnp.tile(sin_s, (BS, 1))

    xp = x_prompt.reshape(B * S, D_MODEL)
    xs = x_sample.reshape(BS * T, D_MODEL)
    L_P = 256
    L_S = 128
    prompt_rows = tuple(min(w, S) for w, _ in A_GROUPS)
    tail = max(prompt_rows)
    dils = tuple(d for _, d in A_GROUPS)
    acc = [[] for _ in range(12)]
    for l in range(depth):
        bf = lambda w: w[l].astype(BF16)
        vec = lambda g: g[l].reshape(1, -1)
        wi = w_in[l]
        w_qkv = jnp.concatenate([wi[:, _C_QM:_C_IG], wi[:, _C_QA:_C_GM]], axis=1).astype(BF16)
        w_gate_t = wi[:, _C_IG:_C_OG].T.astype(BF16)
        w_z = jnp.concatenate([wi[:, _C_OG:_C_QA], wi[:, _C_GM:_C_END]], axis=1).astype(BF16)
        ffn1 = (vec(g_ffn1), bf(w1_gate), bf(w1_up), bf(w1_down))
        ffn2 = (vec(g_ffn2), bf(w2_gate), bf(w2_up), bf(w2_down))
        post_w = (vec(g_mix), vec(g_mhead), w_z, bf(w_up_m), bf(w_up_a), bf(w_out))
        gate_bias = jnp.concatenate([b_igate[l], b_fgate[l]]).reshape(8, 1)
        last = l == depth - 1
        gfin = g_final.reshape(1, -1) if last else None

        xp = _ffn(xp, *ffn1)
        qm, km, vm, *qkv_a, kvf, gt = _inproj(xp, vec(g_mix), w_qkv, w_gate_t, cos_p, sin_p, B, S, tail, dils)
        hm, c_p, n_p, m_p = _mlstm(
            qm.reshape(B, S, M_WIDTH), km.reshape(B, S, M_WIDTH), vm.reshape(B, S, M_WIDTH), gt, gate_bias,
            jnp.zeros((B, M_HEADS, M_DV, M_DK), F32), jnp.zeros((B, M_HEADS, M_DK), F32),
            jnp.zeros((B, M_HEADS, 1), F32), L_P)
        y_a = _attn_prompt(qkv_a[0:3], qkv_a[3:6], qkv_a[6:9], B, S)
        xp = _post(xp, hm.reshape(B * S, M_WIDTH), y_a, *post_w)
        xp = _ffn(xp, *ffn2, g_final=gfin)
        kv_p = _window_rows(kvf, B, prompt_rows)
        for i, t in enumerate((c_p, n_p, m_p.reshape(B, M_HEADS), *kv_p)):
            acc[i].append(t)

        xs = _ffn(xs, *ffn1)
        qm, km, vm, *qkv_a, kvf, gt = _inproj(xs, vec(g_mix), w_qkv, w_gate_t, cos_s, sin_s, 1, BS * T, BS * T,
                                              (1,) * len(A_GROUPS))
        pad_tok = lambda t: jnp.pad(t.reshape(BS, T, M_WIDTH), ((0, 0), (0, L_S - T), (0, 0)))
        gt = gt.reshape(8, BS, T)
        gt = jnp.concatenate([jnp.pad(gt[0:4], ((0, 0), (0, 0), (0, L_S - T)), constant_values=-1e30),
                              jnp.pad(gt[4:8], ((0, 0), (0, 0), (0, L_S - T)), constant_values=1e30)], axis=0)
        hm, c_s, n_s, m_s = _mlstm(
            pad_tok(qm), pad_tok(km), pad_tok(vm), gt.reshape(8, BS * L_S), gate_bias,
            state_mlstm_C[l], state_mlstm_n[l], state_mlstm_m[l].reshape(BS, M_HEADS, 1), L_S)
        hm = hm[:, :T].reshape(BS * T, M_WIDTH)
        caches = [jnp.transpose(c[l], (0, 2, 3, 4, 1)).reshape(BS, 2 * A_GW, c.shape[2]) for c in caches_all]
        q_s = [q.astype(F32).reshape(BS, T, A_GW) for q in qkv_a[0:3]]
        y_a = _attn_cached(q_s, kvf.reshape(BS, T, 2 * A_WIDTH), caches)
        xs = _post(xs, hm, y_a.reshape(BS * T, A_GW // 128, 128).transpose(1, 0, 2), *post_w)
        xs = _ffn(xs, *ffn2, g_final=gfin)
        kv_s = _window_rows(kvf.reshape(BS, T, 2 * A_WIDTH), BS, (T,) * len(A_GROUPS))
        for i, t in enumerate((c_s, n_s, m_s.reshape(BS, M_HEADS), *kv_s)):
            acc[6 + i].append(t)

    stacked = [jnp.stack(a) for a in acc]
    return (xp.reshape(B, S, D_MODEL), xs.reshape(BS, T, D_MODEL), *stacked)
```

```python
import functools

import jax
import jax.numpy as jnp
from jax import lax
from jax.experimental import pallas as pl
from jax.experimental.pallas import tpu as pltpu

F32 = jnp.float32
BF16 = jnp.bfloat16

D_MODEL = 1024
D_FF = 2816
PAST_LEN = 8192
M_HEADS = 4
M_DK = 128
M_DV = 128
M_WIDTH = M_HEADS * M_DV
A_GROUPS = ((128, 1), (512, 4), (2048, 16))
A_HPG = 4
A_HD = 64
A_GW = A_HPG * A_HD
A_WIDTH = len(A_GROUPS) * A_GW
A_BLOCK = 128
ROPE_THETA = 10000.0
EPS = 1e-6
NEG_INF = float("-inf")

_C_QM, _C_KM, _C_VM, _C_IG, _C_FG, _C_OG = 0, 512, 1024, 1536, 1540, 1544
_C_QA, _C_KA, _C_VA, _C_GM, _C_GA, _C_END = 2056, 2824, 3592, 4360, 5384, 6408

VMEM_LIMIT = 56 * 1024 * 1024
NT_DIMS = (((1,), (1,)), ((), ()))
TN_DIMS = (((0,), (0,)), ((), ()))


def _params(n_axes):
    return pltpu.CompilerParams(dimension_semantics=("arbitrary",) * n_axes,
                                vmem_limit_bytes=VMEM_LIMIT)


def _resident(shape):
    return pl.BlockSpec(shape, lambda *_: (0,) * len(shape), pipeline_mode=pl.Buffered(1))


def _rms(x, g):
    return x * lax.rsqrt(jnp.mean(x * x, axis=-1, keepdims=True) + EPS) * g


def _dot(a, b):
    return jnp.dot(a, b, preferred_element_type=F32)


FF_CHUNK = 256


def _ffn_kernel(x_ref, g_ref, wg_ref, wu_ref, wd_ref, *rest, final_norm):
    if final_norm:
        gf_ref, o_ref, act_ref = rest
    else:
        o_ref, act_ref = rest
    x = x_ref[...]
    h = _rms(x, g_ref[...]).astype(BF16)
    for c in range(D_FF // FF_CHUNK):
        sl = slice(c * FF_CHUNK, (c + 1) * FF_CHUNK)
        gate = _dot(h, wg_ref[:, sl])
        up = _dot(h, wu_ref[:, sl])
        act_ref[:, sl] = (gate * jax.nn.sigmoid(gate) * up).astype(BF16)
    out = x + 0.5 * _dot(act_ref[...], wd_ref[...])
    if final_norm:
        out = _rms(out, gf_ref[...])
    o_ref[...] = out


def _ffn(x, g, wg, wu, wd, g_final=None, tm=1024):
    n = x.shape[0]
    tm = min(tm, n)
    final_norm = g_final is not None
    row = pl.BlockSpec((tm, D_MODEL), lambda i: (i, 0))
    in_specs = [row, _resident((1, D_MODEL)), _resident((D_MODEL, D_FF)),
                _resident((D_MODEL, D_FF)), _resident((D_FF, D_MODEL))]
    args = [x, g, wg, wu, wd]
    if final_norm:
        in_specs.append(_resident((1, D_MODEL)))
        args.append(g_final)
    return pl.pallas_call(
        functools.partial(_ffn_kernel, final_norm=final_norm),
        grid=(n // tm,),
        in_specs=in_specs,
        out_specs=row,
        out_shape=jax.ShapeDtypeStruct((n, D_MODEL), F32),
        scratch_shapes=[pltpu.VMEM((tm, D_FF), BF16)],
        compiler_params=_params(1),
        name="ffn_final" if final_norm else "ffn",
    )(*args)


def _rope(t, cos, sin):
    lane = lax.broadcasted_iota(jnp.int32, cos.shape, 1)
    first_half = (lane % A_HD) < (A_HD // 2)
    outs = []
    for c in range(t.shape[1] // 128):
        tc = t[:, c * 128:(c + 1) * 128]
        partner = jnp.where(first_half, pltpu.roll(tc, 128 - A_HD // 2, 1), pltpu.roll(tc, A_HD // 2, 1))
        outs.append(tc * cos + partner * sin)
    return jnp.concatenate(outs, axis=1)


def _inproj_kernel(x_ref, g_ref, w_ref, wgt_ref, cos_ref, sin_ref,
                   qm_ref, km_ref, vm_ref, q1_ref, q2_ref, q3_ref, k1_ref, k2_ref, k3_ref,
                   v1_ref, v2_ref, v3_ref, kvf_ref, gt_ref, z_ref, *, dils):
    tm = x_ref.shape[0]

    def emit(slot, z, refs):
        for c in range(A_WIDTH // 128):
            z_ref[slot, c] = z[:, c * 128:(c + 1) * 128]
        per_group = A_GW // 128
        for g, (ref, d) in enumerate(zip(refs, dils)):
            for r in range(d):
                for p in range(per_group):
                    ref[0, r, :, p * 128:(p + 1) * 128] = z_ref[
                        slot, g * per_group + p, pl.ds(r, tm // d, stride=d), :].astype(BF16)

    h = _rms(x_ref[...], g_ref[...]).astype(BF16)
    qm_ref[...] = _dot(h, w_ref[:, 0:512]).astype(BF16)
    km_ref[...] = (_dot(h, w_ref[:, 512:1024]) * (M_DK ** -0.5)).astype(BF16)
    vm_ref[...] = _dot(h, w_ref[:, 1024:1536]).astype(BF16)
    cos = cos_ref[...]
    sin = sin_ref[...]
    emit(0, _rope(_dot(h, w_ref[:, 1536:2304]), cos, sin) * (A_HD ** -0.5), (q1_ref, q2_ref, q3_ref))
    ka = _rope(_dot(h, w_ref[:, 2304:3072]), cos, sin)
    emit(1, ka, (k1_ref, k2_ref, k3_ref))
    va = _dot(h, w_ref[:, 3072:3840])
    emit(2, va, (v1_ref, v2_ref, v3_ref))
    kvf_ref[0, :, 0:A_WIDTH] = ka
    kvf_ref[0, :, A_WIDTH:2 * A_WIDTH] = va
    gt_ref[0] = lax.dot_general(wgt_ref[...], h, NT_DIMS, preferred_element_type=F32)


def _inproj(x, g, w_qkv, w_gate_t, cos, sin, nbatch, seq, tail, dils, tm=512):
    n = x.shape[0]
    tm = min(tm, tail)
    tpb = seq // tm
    first_tail = tpb - tail // tm
    row = lambda w: pl.BlockSpec((tm, w), lambda i: (i, 0))
    table = pl.BlockSpec((tm, 128), lambda i: (i % tpb, 0))
    grouped_shape = [jax.ShapeDtypeStruct((nbatch, d, seq // d, A_GW), BF16) for d in dils]
    grouped_spec = [pl.BlockSpec((1, d, tm // d, A_GW), lambda i: (i // tpb, 0, i % tpb, 0)) for d in dils]
    out_shapes = (
        jax.ShapeDtypeStruct((n, 512), BF16), jax.ShapeDtypeStruct((n, 512), BF16),
        jax.ShapeDtypeStruct((n, 512), BF16),
        *grouped_shape, *grouped_shape, *grouped_shape,
        jax.ShapeDtypeStruct((nbatch, tail, 2 * A_WIDTH), F32),
        jax.ShapeDtypeStruct((nbatch, 8, seq), F32),
    )
    out_specs = (
        row(512), row(512), row(512), *grouped_spec, *grouped_spec, *grouped_spec,
        pl.BlockSpec((1, tm, 2 * A_WIDTH), lambda i: (i // tpb, jnp.maximum(i % tpb - first_tail, 0), 0)),
        pl.BlockSpec((1, 8, tm), lambda i: (i // tpb, 0, i % tpb)),
    )
    return pl.pallas_call(
        functools.partial(_inproj_kernel, dils=dils),
        grid=(n // tm,),
        in_specs=[row(D_MODEL), _resident((1, D_MODEL)), _resident(w_qkv.shape),
                  _resident(w_gate_t.shape), table, table],
        out_specs=out_specs,
        out_shape=out_shapes,
        scratch_shapes=[pltpu.VMEM((3, A_WIDTH // 128, tm, 128), F32)],
        compiler_params=_params(1),
        name="inproj",
    )(x, g, w_qkv, w_gate_t, cos, sin)


def _scan_lanes(xs, lane, op, identity):
    shift = 1
    while shift < xs[0].shape[1]:
        xs = [op(x, jnp.where(lane >= shift, pltpu.roll(x, shift, 1), identity)) for x in xs]
        shift *= 2
    return xs


def _mlstm_kernel(q_ref, k_ref, v_ref, gt_ref, bias_ref, c0_ref, n0_ref, m0_ref,
                  h_ref, c_ref, n_ref, m_ref, *, L):
    @pl.when(pl.program_id(1) == 0)
    def _():
        c_ref[...] = c0_ref[...]
        n_ref[...] = n0_ref[...]
        m_ref[...] = m0_ref[...]

    seqs = range(q_ref.shape[0])
    units = [(nb, h) for nb in seqs for h in range(M_HEADS)]
    uid = range(len(units))
    hs = lambda h: slice(h * M_DK, (h + 1) * M_DK)
    r_idx = lax.broadcasted_iota(jnp.int32, (L, L), 0)
    c_idx = lax.broadcasted_iota(jnp.int32, (L, L), 1)
    causal = c_idx <= r_idx
    lane = lax.broadcasted_iota(jnp.int32, (M_HEADS, L), 1)
    bias = bias_ref[...]

    gates = [gt_ref[nb] + bias for nb in seqs]
    ig = [g[0:4, :] for g in gates]
    lf = [jnp.minimum(g[4:8, :], 0.0) - jnp.log1p(jnp.exp(-jnp.abs(g[4:8, :]))) for g in gates]
    b = _scan_lanes(lf, lane, jnp.add, 0.0)
    a = [ig[nb] - b[nb] for nb in seqs]
    m_prev = [m_ref[nb] for nb in seqs]
    a_max = _scan_lanes(a, lane, jnp.maximum, NEG_INF)
    big_m = [jnp.maximum(m_prev[nb], a_max[nb]) for nb in seqs]
    m_t = [b[nb] + big_m[nb] for nb in seqs]
    w_inter = [jnp.exp(m_prev[nb] - big_m[nb]) for nb in seqs]
    e_negm = [jnp.exp(-m_t[nb]) for nb in seqs]
    b_last = [b[nb][:, L - 1:L] for nb in seqs]
    m_new = [m_t[nb][:, L - 1:L] for nb in seqs]
    decay = [jnp.exp(b_last[nb] + m_prev[nb] - m_new[nb]) for nb in seqs]
    w_s = [jnp.exp(a[nb] + (b_last[nb] - m_new[nb])) for nb in seqs]
    pad = jnp.zeros((128 - 4 * M_HEADS, L), F32)
    cols = [jnp.transpose(jnp.concatenate([big_m[nb], w_inter[nb], e_negm[nb], w_s[nb], pad], axis=0)) for nb in seqs]
    col = lambda j: [cols[nb][:, 4 * j + h:4 * j + h + 1] for nb, h in units]
    big_m_col, w_inter_col, e_negm_col, w_s_col = col(0), col(1), col(2), col(3)

    q = [q_ref[nb, :, hs(h)] for nb, h in units]
    k = [k_ref[nb, :, hs(h)] for nb, h in units]
    v = [v_ref[nb, :, hs(h)] for nb, h in units]
    c_old = [c_ref[nb, h] for nb, h in units]
    n_old = [n_ref[nb, h:h + 1, :] for nb, h in units]
    s = [lax.dot_general(q[i], k[i], NT_DIMS, preferred_element_type=F32) for i in uid]
    inter = [lax.dot_general(q[i], c_old[i].astype(BF16), NT_DIMS, preferred_element_type=F32) for i in uid]
    p = [s[i] * jnp.exp(jnp.where(causal, a[nb][h:h + 1, :] - big_m_col[i], NEG_INF))
         for i, (nb, h) in enumerate(units)]
    pv = [_dot(p[i].astype(BF16), v[i]) for i in uid]
    num = [w_inter_col[i] * inter[i] + pv[i] for i in uid]
    qn = [jnp.sum(q[i].astype(F32) * n_old[i], axis=-1, keepdims=True) for i in uid]
    den = [w_inter_col[i] * qn[i] + jnp.sum(p[i], axis=-1, keepdims=True) for i in uid]
    hh = [num[i] / jnp.maximum(jnp.abs(den[i]), e_negm_col[i]) for i in uid]
    hn = [hh[i] * lax.rsqrt(jnp.mean(hh[i] * hh[i], axis=-1, keepdims=True) + EPS) for i in uid]
    for i, (nb, h) in enumerate(units):
        h_ref[nb, :, hs(h)] = hn[i]
    vw = [(v[i].astype(F32) * w_s_col[i]).astype(BF16) for i in uid]
    upd = [lax.dot_general(vw[i], k[i], TN_DIMS, preferred_element_type=F32) for i in uid]
    w_rows = [jnp.broadcast_to(w_s[nb][h:h + 1, :], (8, L)).astype(BF16) for nb, h in units]
    n_upd = [_dot(w_rows[i], k[i])[0:1, :] for i in uid]
    for i, (nb, h) in enumerate(units):
        dec = decay[nb][h:h + 1, :]
        c_ref[nb, h] = dec * c_old[i] + upd[i]
        n_ref[nb, h:h + 1, :] = dec * n_old[i] + n_upd[i]
    for nb in seqs:
        m_ref[nb] = m_new[nb]


def _mlstm(q, k, v, gates, bias, c0, n0, m0, L, nb_step):
    nb, seq, _ = q.shape
    nchunk = seq // L
    tok = pl.BlockSpec((nb_step, L, M_WIDTH), lambda b, c: (b, c, 0))
    st_c = pl.BlockSpec((nb_step, M_HEADS, M_DV, M_DK), lambda b, c: (b, 0, 0, 0))
    st_n = pl.BlockSpec((nb_step, M_HEADS, M_DK), lambda b, c: (b, 0, 0))
    st_m = pl.BlockSpec((nb_step, M_HEADS, 1), lambda b, c: (b, 0, 0))
    return pl.pallas_call(
        functools.partial(_mlstm_kernel, L=L),
        grid=(nb // nb_step, nchunk),
        in_specs=[tok, tok, tok, pl.BlockSpec((nb_step, 8, L), lambda b, c: (b, 0, c)),
                  pl.BlockSpec((8, 1), lambda b, c: (0, 0)), st_c, st_n, st_m],
        out_specs=(tok, st_c, st_n, st_m),
        out_shape=(jax.ShapeDtypeStruct((nb, seq, M_WIDTH), F32),
                   jax.ShapeDtypeStruct(c0.shape, F32), jax.ShapeDtypeStruct(n0.shape, F32),
                   jax.ShapeDtypeStruct(m0.shape, F32)),
        compiler_params=_params(2),
        name="mlstm",
    )(q, k, v, gates, bias, c0, n0, m0)


DIL_MAX = max(d for _, d in A_GROUPS)
TQ = A_BLOCK * DIL_MAX


def _group_weights_combine(o_l):
    mx = functools.reduce(jnp.maximum, [l for _, l in o_l])
    es = [jnp.exp(l - mx) for _, l in o_l]
    num = functools.reduce(jnp.add, [e * o for e, (o, _) in zip(es, o_l)])
    return num / functools.reduce(jnp.add, es)


def _attn_fused_kernel(*refs):
    ng = len(A_GROUPS)
    ins = [refs[5 * g:5 * g + 5] for g in range(ng)]
    y_ref = refs[5 * ng]
    o_scs = refs[5 * ng + 1:5 * ng + 1 + ng]
    l_scs = refs[5 * ng + 1 + ng:]
    not_first = pl.program_id(1) > 0
    q_t = lax.broadcasted_iota(jnp.int32, (2 * A_BLOCK, 2 * A_BLOCK), 0) % A_BLOCK
    key = lax.broadcasted_iota(jnp.int32, (2 * A_BLOCK, 2 * A_BLOCK), 1)
    band = jnp.logical_and(key >= q_t, key <= q_t + A_BLOCK)
    bias = jnp.where(band, 0.0, NEG_INF)
    bias_block0 = jnp.where(jnp.logical_and(band, jnp.logical_or(key >= A_BLOCK, not_first)), 0.0, NEG_INF)
    low = lax.broadcasted_iota(jnp.int32, (A_BLOCK, 2 * A_HD), 1) < A_HD
    batch_nt = (((2,), (2,)), ((0,), (0,)))
    batch_nn = (((2,), (1,)), ((0,), (0,)))

    def head_pairs(qp, ks, vs, mask_bias):
        zero = jnp.zeros_like(qp)
        q2 = jnp.concatenate([jnp.where(low, qp, zero), jnp.where(low, zero, qp)], axis=1)
        s = lax.dot_general(q2, ks, batch_nt, preferred_element_type=F32) + mask_bias
        mx = jnp.max(s, axis=-1, keepdims=True)
        e = jnp.exp(s - mx)
        den = jnp.sum(e, axis=-1, keepdims=True)
        o2 = lax.dot_general(e.astype(BF16), vs, batch_nn, preferred_element_type=F32) / den
        lse = mx + jnp.log(den)
        o = jnp.where(low, o2[:, 0:A_BLOCK], o2[:, A_BLOCK:])
        lb = jnp.where(low, jnp.broadcast_to(lse[:, 0:A_BLOCK], o.shape), jnp.broadcast_to(lse[:, A_BLOCK:], o.shape))
        return o, lb

    for g, (_, d) in enumerate(A_GROUPS):
        q_ref, k_ref, kp_ref, v_ref, vp_ref = ins[g]
        o_sc, l_sc = o_scs[g], l_scs[g]
        nblk = TQ // d // A_BLOCK
        rest = (nblk - 1) * A_BLOCK
        blocks = lambda t: t.reshape(d * (nblk - 1), A_BLOCK, 128)
        for p in range(A_GW // 128):
            cs = slice(p * 128, (p + 1) * 128)
            o, lb = head_pairs(
                q_ref[0, :, 0:A_BLOCK, cs],
                jnp.concatenate([kp_ref[0, :, :, cs], k_ref[0, :, 0:A_BLOCK, cs]], axis=1),
                jnp.concatenate([vp_ref[0, :, :, cs], v_ref[0, :, 0:A_BLOCK, cs]], axis=1), bias_block0)
            o_sc[p, :, 0:A_BLOCK, :] = o
            l_sc[p, :, 0:A_BLOCK, :] = lb
            if nblk > 1:
                o, lb = head_pairs(
                    blocks(q_ref[0, :, A_BLOCK:, cs]),
                    jnp.concatenate([blocks(k_ref[0, :, 0:rest, cs]), blocks(k_ref[0, :, A_BLOCK:, cs])], axis=1),
                    jnp.concatenate([blocks(v_ref[0, :, 0:rest, cs]), blocks(v_ref[0, :, A_BLOCK:, cs])], axis=1), bias)
                o_sc[p, :, A_BLOCK:, :] = o.reshape(d, rest, 128)
                l_sc[p, :, A_BLOCK:, :] = lb.reshape(d, rest, 128)

    for rho in range(DIL_MAX):
        for p in range(A_GW // 128):
            o_l = []
            for g, (_, d) in enumerate(A_GROUPS):
                rows = pl.ds(rho // d, A_BLOCK, stride=DIL_MAX // d) if d < DIL_MAX else slice(None)
                o_l.append((o_scs[g][p, rho % d, rows, :], l_scs[g][p, rho % d, rows, :]))
            y_ref[p, pl.ds(rho, A_BLOCK, stride=DIL_MAX), :] = _group_weights_combine(o_l)


def _attn_prompt(qs, ks, vs, nbatch, seq):
    nt = seq // TQ
    npair = A_GW // 128
    in_specs, args, scratch = [], [], []
    for g, (_, d) in enumerate(A_GROUPS):
        nblk = TQ // d // A_BLOCK
        cur = pl.BlockSpec((1, d, TQ // d, A_GW), lambda b, j: (b, 0, j, 0))
        prev = pl.BlockSpec((1, d, A_BLOCK, A_GW), lambda b, j, nblk=nblk: (b, 0, jnp.maximum(j * nblk - 1, 0), 0))
        in_specs += [cur, cur, prev, cur, prev]
        args += [qs[g], ks[g], ks[g], vs[g], vs[g]]
    for _ in range(2):
        scratch += [pltpu.VMEM((npair, d, TQ // d, 128), F32) for _, d in A_GROUPS]
    return pl.pallas_call(
        _attn_fused_kernel,
        grid=(nbatch, nt),
        in_specs=in_specs,
        out_specs=pl.BlockSpec((npair, TQ, 128), lambda b, j: (0, b * nt + j, 0)),
        out_shape=jax.ShapeDtypeStruct((npair, nbatch * seq, 128), F32),
        scratch_shapes=scratch,
        compiler_params=_params(2),
        name="attn_prompt",
    )(*args)


NEW_PAD = 128


def _attn_cached_kernel(q1_ref, q2_ref, q3_ref, kvn_ref, c1_ref, c2_ref, c3_ref, y_ref, *, T):
    rows = A_HPG * T
    kvn = kvn_ref[0]
    row_head = lax.broadcasted_iota(jnp.int32, (rows, A_GW), 0) // T
    lane_head = lax.broadcasted_iota(jnp.int32, (rows, A_GW), 1) // A_HD
    out_head = lax.broadcasted_iota(jnp.int32, (T, A_GW), 1) // A_HD
    pad = jnp.zeros((NEW_PAD - T, A_GW), F32)
    o_l = []
    for g, (window, dil) in enumerate(A_GROUPS):
        cref = (c1_ref, c2_ref, c3_ref)[g]
        n_cached = cref.shape[2]
        qg = (q1_ref, q2_ref, q3_ref)[g][0]
        qbd = jnp.where(row_head == lane_head, jnp.concatenate([qg] * A_HPG, axis=0), 0.0).astype(BF16)
        k_t = cref[0, 0:A_GW, :].astype(BF16)
        v_t = cref[0, A_GW:2 * A_GW, :].astype(BF16)
        kn = jnp.concatenate([kvn[:, g * A_GW:(g + 1) * A_GW], pad], axis=0).astype(BF16)
        vn = jnp.concatenate([kvn[:, A_WIDTH + g * A_GW:A_WIDTH + (g + 1) * A_GW], pad], axis=0).astype(BF16)
        s_c = _dot(qbd, k_t)
        s_n = lax.dot_general(qbd, kn, NT_DIMS, preferred_element_type=F32)
        t_c = lax.broadcasted_iota(jnp.int32, (rows, n_cached), 0) % T
        jd_c = n_cached + t_c - lax.broadcasted_iota(jnp.int32, (rows, n_cached), 1)
        ok_c = jnp.logical_and(jd_c % dil == 0, jd_c <= window)
        t_n = lax.broadcasted_iota(jnp.int32, (rows, NEW_PAD), 0) % T
        jd_n = t_n - lax.broadcasted_iota(jnp.int32, (rows, NEW_PAD), 1)
        ok_n = jnp.logical_and(jnp.logical_and(jd_n >= 0, jd_n % dil == 0), jd_n <= window)
        s_c = jnp.where(ok_c, s_c, NEG_INF)
        s_n = jnp.where(ok_n, s_n, NEG_INF)
        mx = jnp.maximum(jnp.max(s_c, axis=-1, keepdims=True), jnp.max(s_n, axis=-1, keepdims=True))
        e_c = jnp.exp(s_c - mx)
        e_n = jnp.exp(s_n - mx)
        den = jnp.sum(e_c, axis=-1, keepdims=True) + jnp.sum(e_n, axis=-1, keepdims=True)
        res = (lax.dot_general(e_c.astype(BF16), v_t, NT_DIMS, preferred_element_type=F32)
               + _dot(e_n.astype(BF16), vn)) / den
        lse = mx + jnp.log(den)
        o = jnp.zeros((T, A_GW), F32)
        lb = jnp.zeros((T, A_GW), F32)
        for h in range(A_HPG):
            o = jnp.where(out_head == h, res[h * T:(h + 1) * T, :], o)
            lb = jnp.where(out_head == h, lse[h * T:(h + 1) * T, :], lb)
        o_l.append((o, lb))
    y_ref[0] = _group_weights_combine(o_l)


def _attn_cached(qs, kvn, caches):
    nb, T, _ = kvn.shape
    per_b = lambda t: pl.BlockSpec((1,) + t.shape[1:], lambda b: (b, 0, 0))
    return pl.pallas_call(
        functools.partial(_attn_cached_kernel, T=T),
        grid=(nb,),
        in_specs=[per_b(q) for q in qs] + [per_b(kvn)] + [per_b(c) for c in caches],
        out_specs=pl.BlockSpec((1, T, A_GW), lambda b: (b, 0, 0)),
        out_shape=jax.ShapeDtypeStruct((nb, T, A_GW), F32),
        compiler_params=_params(1),
        name="attn_cached",
    )(*qs, kvn, *caches)


def _post_kernel(x_ref, hm_ref, ya_ref, gmix_ref, gmh_ref, wz_ref, wum_ref, wua_ref, wo_ref, out_ref):
    x = x_ref[...]
    h = _rms(x, gmix_ref[...]).astype(BF16)
    og = _dot(h, wz_ref[:, 0:M_WIDTH])
    gm = _dot(h, wz_ref[:, M_WIDTH:M_WIDTH + D_MODEL])
    ga = _dot(h, wz_ref[:, M_WIDTH + D_MODEL:M_WIDTH + 2 * D_MODEL])
    hm = hm_ref[...] * gmh_ref[...] * jax.nn.sigmoid(og)
    y_m = _dot(hm.astype(BF16), wum_ref[...])
    ya = jnp.concatenate([ya_ref[p] for p in range(ya_ref.shape[0])], axis=1)
    y_a = _dot(ya.astype(BF16), wua_ref[...])
    merged = jax.nn.sigmoid(gm) * y_m + jax.nn.sigmoid(ga) * y_a
    out_ref[...] = x + _dot(merged.astype(BF16), wo_ref[...])


def _post(x, hm, y_a, g_mix, g_mhead, w_z, w_up_m, w_up_a, w_out, tm=512):
    n = x.shape[0]
    tm = min(tm, n)
    row = lambda w: pl.BlockSpec((tm, w), lambda i: (i, 0))
    return pl.pallas_call(
        _post_kernel,
        grid=(n // tm,),
        in_specs=[row(D_MODEL), row(M_WIDTH), pl.BlockSpec((y_a.shape[0], tm, 128), lambda i: (0, i, 0)),
                  _resident((1, D_MODEL)), _resident((1, M_WIDTH)), _resident(w_z.shape),
                  _resident(w_up_m.shape), _resident(w_up_a.shape), _resident(w_out.shape)],
        out_specs=row(D_MODEL),
        out_shape=jax.ShapeDtypeStruct((n, D_MODEL), F32),
        compiler_params=_params(1),
        name="post",
    )(x, hm, y_a, g_mix, g_mhead, w_z, w_up_m, w_up_a, w_out)


def _rope_tables(pos):
    half = A_HD // 2
    inv_freq = ROPE_THETA ** (-2.0 * jnp.arange(half, dtype=F32) / A_HD)
    ang = pos.astype(F32)[:, None] * inv_freq[None, :]
    cos = jnp.cos(ang)
    sin = jnp.sin(ang)
    return jnp.tile(cos, (1, 4)), jnp.tile(jnp.concatenate([-sin, sin], axis=1), (1, 2))


def _window_rows(kvf, nbatch, rows):
    res = []
    for g, r in enumerate(rows):
        kk = kvf[:, kvf.shape[1] - r:, g * A_GW:(g + 1) * A_GW].reshape(nbatch, r, A_HPG, A_HD)
        vv = kvf[:, kvf.shape[1] - r:, A_WIDTH + g * A_GW:A_WIDTH + (g + 1) * A_GW].reshape(nbatch, r, A_HPG, A_HD)
        res.append(jnp.stack([kk, vv], axis=2))
    return res


def kernel(x_prompt, x_sample, state_mlstm_C, state_mlstm_n, state_mlstm_m, cache_win128_kv, cache_win512_kv, cache_win2048_kv, g_ffn1, w1_gate, w1_up, w1_down, g_mix, w_in, b_igate, b_fgate, g_mhead, w_up_m, w_up_a, w_out, g_ffn2, w2_gate, w2_up, w2_down, g_final):
    B, S, _ = x_prompt.shape
    BS, T, _ = x_sample.shape
    depth = w_in.shape[0]
    caches_all = (cache_win128_kv, cache_win512_kv, cache_win2048_kv)

    cos_p, sin_p = _rope_tables(jnp.arange(S, dtype=jnp.int32))
    cos_s, sin_s = _rope_tables(PAST_LEN + jnp.arange(T, dtype=jnp.int32))
    cos_s = jnp.tile(cos_s, (BS, 1))
    sin_s = jnp.tile(sin_s, (BS, 1))

    xp = x_prompt.reshape(B * S, D_MODEL)
    xs = x_sample.reshape(BS * T, D_MODEL)
    L_P = 256
    L_S = 128
    NB_STEP = 4
    prompt_rows = tuple(min(w, S) for w, _ in A_GROUPS)
    tail = max(prompt_rows)
    dils = tuple(d for _, d in A_GROUPS)
    acc = [[] for _ in range(12)]
    for l in range(depth):
        bf = lambda w: w[l].astype(BF16)
        vec = lambda g: g[l].reshape(1, -1)
        wi = w_in[l]
        w_qkv = jnp.concatenate([wi[:, _C_QM:_C_IG], wi[:, _C_QA:_C_GM]], axis=1).astype(BF16)
        w_gate_t = wi[:, _C_IG:_C_OG].T.astype(BF16)
        w_z = jnp.concatenate([wi[:, _C_OG:_C_QA], wi[:, _C_GM:_C_END]], axis=1).astype(BF16)
        ffn1 = (vec(g_ffn1), bf(w1_gate), bf(w1_up), bf(w1_down))
        ffn2 = (vec(g_ffn2), bf(w2_gate), bf(w2_up), bf(w2_down))
        post_w = (vec(g_mix), vec(g_mhead), w_z, bf(w_up_m), bf(w_up_a), bf(w_out))
        gate_bias = jnp.concatenate([b_igate[l], b_fgate[l]]).reshape(8, 1)
        last = l == depth - 1
        gfin = g_final.reshape(1, -1) if last else None

        xp = _ffn(xp, *ffn1)
        qm, km, vm, *qkv_a, kvf, gt = _inproj(xp, vec(g_mix), w_qkv, w_gate_t, cos_p, sin_p, B, S, tail, dils)
        hm, c_p, n_p, m_p = _mlstm(
            qm.reshape(B, S, M_WIDTH), km.reshape(B, S, M_WIDTH), vm.reshape(B, S, M_WIDTH), gt, gate_bias,
            jnp.zeros((B, M_HEADS, M_DV, M_DK), F32), jnp.zeros((B, M_HEADS, M_DK), F32),
            jnp.zeros((B, M_HEADS, 1), F32), L_P, NB_STEP)
        y_a = _attn_prompt(qkv_a[0:3], qkv_a[3:6], qkv_a[6:9], B, S)
        xp = _post(xp, hm.reshape(B * S, M_WIDTH), y_a, *post_w)
        xp = _ffn(xp, *ffn2, g_final=gfin)
        kv_p = _window_rows(kvf, B, prompt_rows)
        for i, t in enumerate((c_p, n_p, m_p.reshape(B, M_HEADS), *kv_p)):
            acc[i].append(t)

        xs = _ffn(xs, *ffn1)
        qm, km, vm, *qkv_a, kvf, gt = _inproj(xs, vec(g_mix), w_qkv, w_gate_t, cos_s, sin_s, 1, BS * T, BS * T,
                                              (1,) * len(A_GROUPS))
        pad_tok = lambda t: jnp.pad(t.reshape(BS, T, M_WIDTH), ((0, 0), (0, L_S - T), (0, 0)))
        gt = gt.reshape(8, BS, T).transpose(1, 0, 2)
        gt = jnp.concatenate([jnp.pad(gt[:, 0:4], ((0, 0), (0, 0), (0, L_S - T)), constant_values=-1e30),
                              jnp.pad(gt[:, 4:8], ((0, 0), (0, 0), (0, L_S - T)), constant_values=1e30)], axis=1)
        hm, c_s, n_s, m_s = _mlstm(
            pad_tok(qm), pad_tok(km), pad_tok(vm), gt, gate_bias,
            state_mlstm_C[l], state_mlstm_n[l], state_mlstm_m[l].reshape(BS, M_HEADS, 1), L_S, NB_STEP)
        hm = hm[:, :T].reshape(BS * T, M_WIDTH)
        caches = [jnp.transpose(c[l], (0, 2, 3, 4, 1)).reshape(BS, 2 * A_GW, c.shape[2]) for c in caches_all]
        q_s = [q.astype(F32).reshape(BS, T, A_GW) for q in qkv_a[0:3]]
        y_a = _attn_cached(q_s, kvf.reshape(BS, T, 2 * A_WIDTH), caches)
        xs = _post(xs, hm, y_a.reshape(BS * T, A_GW // 128, 128).transpose(1, 0, 2), *post_w)
        xs = _ffn(xs, *ffn2, g_final=gfin)
        kv_s = _window_rows(kvf.reshape(BS, T, 2 * A_WIDTH), BS, (T,) * len(A_GROUPS))
        for i, t in enumerate((c_s, n_s, m_s.reshape(BS, M_HEADS), *kv_s)):
            acc[6 + i].append(t)

    stacked = [jnp.stack(a) for a in acc]
    return (xp.reshape(B, S, D_MODEL), xs.reshape(BS, T, D_MODEL), *stacked)
```

```python
import functools

import jax
import jax.numpy as jnp
import numpy as np
from jax import lax
from jax.experimental import pallas as pl
from jax.experimental.pallas import tpu as pltpu

F32 = jnp.float32
BF16 = jnp.bfloat16

D_MODEL = 1024
D_FF = 2816
PAST_LEN = 8192
M_HEADS = 4
M_DK = 128
M_DV = 128
M_WIDTH = M_HEADS * M_DV
A_GROUPS = ((128, 1), (512, 4), (2048, 16))
A_HPG = 4
A_HD = 64
A_GW = A_HPG * A_HD
A_WIDTH = len(A_GROUPS) * A_GW
A_BLOCK = 128
ROPE_THETA = 10000.0
EPS = 1e-6
NEG_INF = float("-inf")

_C_QM, _C_KM, _C_VM, _C_IG, _C_FG, _C_OG = 0, 512, 1024, 1536, 1540, 1544
_C_QA, _C_KA, _C_VA, _C_GM, _C_GA, _C_END = 2056, 2824, 3592, 4360, 5384, 6408

VMEM_LIMIT = 56 * 1024 * 1024
NT_DIMS = (((1,), (1,)), ((), ()))
TN_DIMS = (((0,), (0,)), ((), ()))


def _params(n_axes):
    return pltpu.CompilerParams(dimension_semantics=("arbitrary",) * n_axes,
                                vmem_limit_bytes=VMEM_LIMIT)


def _resident(shape):
    return pl.BlockSpec(shape, lambda *_: (0,) * len(shape), pipeline_mode=pl.Buffered(1))


def _rms(x, g):
    return x * lax.rsqrt(jnp.mean(x * x, axis=-1, keepdims=True) + EPS) * g


def _dot(a, b):
    return jnp.dot(a, b, preferred_element_type=F32)


FF_CHUNK = 256


def _ffn_kernel(x_ref, g_ref, wg_ref, wu_ref, wd_ref, *rest, final_norm):
    if final_norm:
        gf_ref, o_ref, act_ref = rest
    else:
        o_ref, act_ref = rest
    x = x_ref[...]
    h = _rms(x, g_ref[...]).astype(BF16)
    for c in range(D_FF // FF_CHUNK):
        sl = slice(c * FF_CHUNK, (c + 1) * FF_CHUNK)
        gate = _dot(h, wg_ref[:, sl])
        up = _dot(h, wu_ref[:, sl])
        act_ref[:, sl] = (gate * jax.nn.sigmoid(gate) * up).astype(BF16)
    out = x + 0.5 * _dot(act_ref[...], wd_ref[...])
    if final_norm:
        out = _rms(out, gf_ref[...])
    o_ref[...] = out


def _ffn(x, g, wg, wu, wd, g_final=None, tm=1024):
    n = x.shape[0]
    tm = min(tm, n)
    final_norm = g_final is not None
    row = pl.BlockSpec((tm, D_MODEL), lambda i: (i, 0))
    in_specs = [row, _resident((1, D_MODEL)), _resident((D_MODEL, D_FF)),
                _resident((D_MODEL, D_FF)), _resident((D_FF, D_MODEL))]
    args = [x, g, wg, wu, wd]
    if final_norm:
        in_specs.append(_resident((1, D_MODEL)))
        args.append(g_final)
    return pl.pallas_call(
        functools.partial(_ffn_kernel, final_norm=final_norm),
        grid=(n // tm,),
        in_specs=in_specs,
        out_specs=row,
        out_shape=jax.ShapeDtypeStruct((n, D_MODEL), F32),
        scratch_shapes=[pltpu.VMEM((tm, D_FF), BF16)],
        compiler_params=_params(1),
        name="ffn_final" if final_norm else "ffn",
    )(*args)


def _rope(t, cos, sin):
    lane = lax.broadcasted_iota(jnp.int32, cos.shape, 1)
    first_half = (lane % A_HD) < (A_HD // 2)
    outs = []
    for c in range(t.shape[1] // 128):
        tc = t[:, c * 128:(c + 1) * 128]
        partner = jnp.where(first_half, pltpu.roll(tc, 128 - A_HD // 2, 1), pltpu.roll(tc, A_HD // 2, 1))
        outs.append(tc * cos + partner * sin)
    return jnp.concatenate(outs, axis=1)


def _inproj_kernel(x_ref, g_ref, w_ref, wgt_ref, cos_ref, sin_ref,
                   qm_ref, km_ref, vm_ref, q1_ref, q2_ref, q3_ref, k1_ref, k2_ref, k3_ref,
                   v1_ref, v2_ref, v3_ref, kvf_ref, gt_ref, z_ref, *, dils):
    tm = x_ref.shape[0]

    def emit(z, refs):
        for c in range(A_WIDTH // 128):
            z_ref[c] = z[:, c * 128:(c + 1) * 128]
        per_group = A_GW // 128
        for g, (ref, d) in enumerate(zip(refs, dils)):
            for r in range(d):
                for p in range(per_group):
                    ref[0, r, :, p * 128:(p + 1) * 128] = z_ref[
                        g * per_group + p, pl.ds(r, tm // d, stride=d), :].astype(BF16)

    h = _rms(x_ref[...], g_ref[...]).astype(BF16)
    qm_ref[...] = _dot(h, w_ref[:, 0:512]).astype(BF16)
    km_ref[...] = (_dot(h, w_ref[:, 512:1024]) * (M_DK ** -0.5)).astype(BF16)
    vm_ref[...] = _dot(h, w_ref[:, 1024:1536]).astype(BF16)
    cos = cos_ref[...]
    sin = sin_ref[...]
    emit(_rope(_dot(h, w_ref[:, 1536:2304]), cos, sin) * (A_HD ** -0.5), (q1_ref, q2_ref, q3_ref))
    ka = _rope(_dot(h, w_ref[:, 2304:3072]), cos, sin)
    emit(ka, (k1_ref, k2_ref, k3_ref))
    va = _dot(h, w_ref[:, 3072:3840])
    emit(va, (v1_ref, v2_ref, v3_ref))
    kvf_ref[0, :, 0:A_WIDTH] = ka
    kvf_ref[0, :, A_WIDTH:2 * A_WIDTH] = va
    gt_ref[0] = lax.dot_general(wgt_ref[...], h, NT_DIMS, preferred_element_type=F32)


def _inproj(x, g, w_qkv, w_gate_t, cos, sin, nbatch, seq, tail, dils, tm=1024):
    n = x.shape[0]
    tm = min(tm, tail)
    tpb = seq // tm
    first_tail = tpb - tail // tm
    row = lambda w: pl.BlockSpec((tm, w), lambda i: (i, 0))
    table = pl.BlockSpec((tm, 128), lambda i: (i % tpb, 0))
    grouped_shape = [jax.ShapeDtypeStruct((nbatch, d, seq // d, A_GW), BF16) for d in dils]
    grouped_spec = [pl.BlockSpec((1, d, tm // d, A_GW), lambda i: (i // tpb, 0, i % tpb, 0)) for d in dils]
    out_shapes = (
        jax.ShapeDtypeStruct((n, 512), BF16), jax.ShapeDtypeStruct((n, 512), BF16),
        jax.ShapeDtypeStruct((n, 512), BF16),
        *grouped_shape, *grouped_shape, *grouped_shape,
        jax.ShapeDtypeStruct((nbatch, tail, 2 * A_WIDTH), F32),
        jax.ShapeDtypeStruct((nbatch, 8, seq), F32),
    )
    out_specs = (
        row(512), row(512), row(512), *grouped_spec, *grouped_spec, *grouped_spec,
        pl.BlockSpec((1, tm, 2 * A_WIDTH), lambda i: (i // tpb, jnp.maximum(i % tpb - first_tail, 0), 0)),
        pl.BlockSpec((1, 8, tm), lambda i: (i // tpb, 0, i % tpb)),
    )
    return pl.pallas_call(
        functools.partial(_inproj_kernel, dils=dils),
        grid=(n // tm,),
        in_specs=[row(D_MODEL), _resident((1, D_MODEL)), _resident(w_qkv.shape),
                  _resident(w_gate_t.shape), table, table],
        out_specs=out_specs,
        out_shape=out_shapes,
        scratch_shapes=[pltpu.VMEM((A_WIDTH // 128, tm, 128), F32)],
        compiler_params=_params(1),
        name="inproj",
    )(x, g, w_qkv, w_gate_t, cos, sin)


def _scan_lanes(xs, lane, op, identity):
    shift = 1
    while shift < xs[0].shape[1]:
        xs = [op(x, jnp.where(lane >= shift, pltpu.roll(x, shift, 1), identity)) for x in xs]
        shift *= 2
    return xs


def _mlstm_kernel(q_ref, k_ref, v_ref, gt_ref, bias_ref, c0_ref, n0_ref, m0_ref,
                  h_ref, c_ref, n_ref, m_ref, *, L):
    @pl.when(pl.program_id(1) == 0)
    def _():
        c_ref[...] = c0_ref[...]
        n_ref[...] = n0_ref[...]
        m_ref[...] = m0_ref[...]

    seqs = range(q_ref.shape[0])
    units = [(nb, h) for nb in seqs for h in range(M_HEADS)]
    uid = range(len(units))
    hs = lambda h: slice(h * M_DK, (h + 1) * M_DK)
    r_idx = lax.broadcasted_iota(jnp.int32, (L, L), 0)
    c_idx = lax.broadcasted_iota(jnp.int32, (L, L), 1)
    causal = c_idx <= r_idx
    lane = lax.broadcasted_iota(jnp.int32, (M_HEADS, L), 1)
    bias = bias_ref[...]

    gates = [gt_ref[nb] + bias for nb in seqs]
    ig = [g[0:4, :] for g in gates]
    lf = [jnp.minimum(g[4:8, :], 0.0) - jnp.log1p(jnp.exp(-jnp.abs(g[4:8, :]))) for g in gates]
    b = _scan_lanes(lf, lane, jnp.add, 0.0)
    a = [ig[nb] - b[nb] for nb in seqs]
    m_prev = [m_ref[nb] for nb in seqs]
    a_max = _scan_lanes(a, lane, jnp.maximum, NEG_INF)
    big_m = [jnp.maximum(m_prev[nb], a_max[nb]) for nb in seqs]
    m_t = [b[nb] + big_m[nb] for nb in seqs]
    w_inter = [jnp.exp(m_prev[nb] - big_m[nb]) for nb in seqs]
    e_negm = [jnp.exp(-m_t[nb]) for nb in seqs]
    b_last = [b[nb][:, L - 1:L] for nb in seqs]
    m_new = [m_t[nb][:, L - 1:L] for nb in seqs]
    decay = [jnp.exp(b_last[nb] + m_prev[nb] - m_new[nb]) for nb in seqs]
    w_s = [jnp.exp(a[nb] + (b_last[nb] - m_new[nb])) for nb in seqs]
    pad = jnp.zeros((128 - 4 * M_HEADS, L), F32)
    cols = [jnp.transpose(jnp.concatenate([big_m[nb], w_inter[nb], e_negm[nb], w_s[nb], pad], axis=0)) for nb in seqs]
    col = lambda j: [cols[nb][:, 4 * j + h:4 * j + h + 1] for nb, h in units]
    big_m_col, w_inter_col, e_negm_col, w_s_col = col(0), col(1), col(2), col(3)

    q = [q_ref[nb, :, hs(h)] for nb, h in units]
    k = [k_ref[nb, :, hs(h)] for nb, h in units]
    v = [v_ref[nb, :, hs(h)] for nb, h in units]
    c_old = [c_ref[nb, h] for nb, h in units]
    n_old = [n_ref[nb, h:h + 1, :] for nb, h in units]
    s = [lax.dot_general(q[i], k[i], NT_DIMS, preferred_element_type=F32) for i in uid]
    inter = [lax.dot_general(q[i], c_old[i].astype(BF16), NT_DIMS, preferred_element_type=F32) for i in uid]
    p = [s[i] * jnp.exp(jnp.where(causal, a[nb][h:h + 1, :] - big_m_col[i], NEG_INF))
         for i, (nb, h) in enumerate(units)]
    pv = [_dot(p[i].astype(BF16), v[i]) for i in uid]
    num = [w_inter_col[i] * inter[i] + pv[i] for i in uid]
    qn = [jnp.sum(q[i].astype(F32) * n_old[i], axis=-1, keepdims=True) for i in uid]
    den = [w_inter_col[i] * qn[i] + jnp.sum(p[i], axis=-1, keepdims=True) for i in uid]
    hh = [num[i] / jnp.maximum(jnp.abs(den[i]), e_negm_col[i]) for i in uid]
    hn = [hh[i] * lax.rsqrt(jnp.mean(hh[i] * hh[i], axis=-1, keepdims=True) + EPS) for i in uid]
    for i, (nb, h) in enumerate(units):
        h_ref[nb, :, hs(h)] = hn[i]
    vw = [(v[i].astype(F32) * w_s_col[i]).astype(BF16) for i in uid]
    upd = [lax.dot_general(vw[i], k[i], TN_DIMS, preferred_element_type=F32) for i in uid]
    w_rows = [jnp.broadcast_to(w_s[nb][h:h + 1, :], (8, L)).astype(BF16) for nb, h in units]
    n_upd = [_dot(w_rows[i], k[i])[0:1, :] for i in uid]
    for i, (nb, h) in enumerate(units):
        dec = decay[nb][h:h + 1, :]
        c_ref[nb, h] = dec * c_old[i] + upd[i]
        n_ref[nb, h:h + 1, :] = dec * n_old[i] + n_upd[i]
    for nb in seqs:
        m_ref[nb] = m_new[nb]


def _mlstm(q, k, v, gates, bias, c0, n0, m0, L, nb_step):
    nb, seq, _ = q.shape
    nchunk = seq // L
    tok = pl.BlockSpec((nb_step, L, M_WIDTH), lambda b, c: (b, c, 0))
    st_c = pl.BlockSpec((nb_step, M_HEADS, M_DV, M_DK), lambda b, c: (b, 0, 0, 0))
    st_n = pl.BlockSpec((nb_step, M_HEADS, M_DK), lambda b, c: (b, 0, 0))
    st_m = pl.BlockSpec((nb_step, M_HEADS, 1), lambda b, c: (b, 0, 0))
    return pl.pallas_call(
        functools.partial(_mlstm_kernel, L=L),
        grid=(nb // nb_step, nchunk),
        in_specs=[tok, tok, tok, pl.BlockSpec((nb_step, 8, L), lambda b, c: (b, 0, c)),
                  pl.BlockSpec((8, 1), lambda b, c: (0, 0)), st_c, st_n, st_m],
        out_specs=(tok, st_c, st_n, st_m),
        out_shape=(jax.ShapeDtypeStruct((nb, seq, M_WIDTH), F32),
                   jax.ShapeDtypeStruct(c0.shape, F32), jax.ShapeDtypeStruct(n0.shape, F32),
                   jax.ShapeDtypeStruct(m0.shape, F32)),
        compiler_params=_params(2),
        name="mlstm",
    )(q, k, v, gates, bias, c0, n0, m0)


DIL_MAX = max(d for _, d in A_GROUPS)
TQ = A_BLOCK * DIL_MAX


def _group_weights_combine(o_l):
    mx = functools.reduce(jnp.maximum, [l for _, l in o_l])
    es = [jnp.exp(l - mx) for _, l in o_l]
    num = functools.reduce(jnp.add, [e * o for e, (o, _) in zip(es, o_l)])
    return num / functools.reduce(jnp.add, es)


def _attn_fused_kernel(*refs):
    ng = len(A_GROUPS)
    ins = [refs[5 * g:5 * g + 5] for g in range(ng)]
    y_ref = refs[5 * ng]
    o_scs = refs[5 * ng + 1:5 * ng + 1 + ng]
    l_scs = refs[5 * ng + 1 + ng:]
    not_first = pl.program_id(1) > 0
    q_t = lax.broadcasted_iota(jnp.int32, (2 * A_BLOCK, 2 * A_BLOCK), 0) % A_BLOCK
    key = lax.broadcasted_iota(jnp.int32, (2 * A_BLOCK, 2 * A_BLOCK), 1)
    band = jnp.logical_and(key >= q_t, key <= q_t + A_BLOCK)
    bias = jnp.where(band, 0.0, NEG_INF)
    bias_block0 = jnp.where(jnp.logical_and(band, jnp.logical_or(key >= A_BLOCK, not_first)), 0.0, NEG_INF)
    low = lax.broadcasted_iota(jnp.int32, (A_BLOCK, 2 * A_HD), 1) < A_HD
    batch_nt = (((2,), (2,)), ((0,), (0,)))
    batch_nn = (((2,), (1,)), ((0,), (0,)))

    def head_pairs(qp, ks, vs, mask_bias):
        zero = jnp.zeros_like(qp)
        q2 = jnp.concatenate([jnp.where(low, qp, zero), jnp.where(low, zero, qp)], axis=1)
        s = lax.dot_general(q2, ks, batch_nt, preferred_element_type=F32) + mask_bias
        mx = jnp.max(s, axis=-1, keepdims=True)
        e = jnp.exp(s - mx)
        den = jnp.sum(e, axis=-1, keepdims=True)
        o2 = lax.dot_general(e.astype(BF16), vs, batch_nn, preferred_element_type=F32) / den
        lse = mx + jnp.log(den)
        o = jnp.where(low, o2[:, 0:A_BLOCK], o2[:, A_BLOCK:])
        lb = jnp.where(low, jnp.broadcast_to(lse[:, 0:A_BLOCK], o.shape), jnp.broadcast_to(lse[:, A_BLOCK:], o.shape))
        return o, lb

    for g, (_, d) in enumerate(A_GROUPS):
        q_ref, k_ref, kp_ref, v_ref, vp_ref = ins[g]
        o_sc, l_sc = o_scs[g], l_scs[g]
        nblk = TQ // d // A_BLOCK
        rest = (nblk - 1) * A_BLOCK
        blocks = lambda t: t.reshape(d * (nblk - 1), A_BLOCK, 128)
        for p in range(A_GW // 128):
            cs = slice(p * 128, (p + 1) * 128)
            o, lb = head_pairs(
                q_ref[0, :, 0:A_BLOCK, cs],
                jnp.concatenate([kp_ref[0, :, :, cs], k_ref[0, :, 0:A_BLOCK, cs]], axis=1),
                jnp.concatenate([vp_ref[0, :, :, cs], v_ref[0, :, 0:A_BLOCK, cs]], axis=1), bias_block0)
            o_sc[p, :, 0:A_BLOCK, :] = o
            l_sc[p, :, 0:A_BLOCK, :] = lb
            if nblk > 1:
                o, lb = head_pairs(
                    blocks(q_ref[0, :, A_BLOCK:, cs]),
                    jnp.concatenate([blocks(k_ref[0, :, 0:rest, cs]), blocks(k_ref[0, :, A_BLOCK:, cs])], axis=1),
                    jnp.concatenate([blocks(v_ref[0, :, 0:rest, cs]), blocks(v_ref[0, :, A_BLOCK:, cs])], axis=1), bias)
                o_sc[p, :, A_BLOCK:, :] = o.reshape(d, rest, 128)
                l_sc[p, :, A_BLOCK:, :] = lb.reshape(d, rest, 128)

    for rho in range(DIL_MAX):
        for p in range(A_GW // 128):
            o_l = []
            for g, (_, d) in enumerate(A_GROUPS):
                rows = pl.ds(rho // d, A_BLOCK, stride=DIL_MAX // d) if d < DIL_MAX else slice(None)
                o_l.append((o_scs[g][p, rho % d, rows, :], l_scs[g][p, rho % d, rows, :]))
            y_ref[p, pl.ds(rho, A_BLOCK, stride=DIL_MAX), :] = _group_weights_combine(o_l)


def _attn_prompt(qs, ks, vs, nbatch, seq):
    nt = seq // TQ
    npair = A_GW // 128
    in_specs, args, scratch = [], [], []
    for g, (_, d) in enumerate(A_GROUPS):
        nblk = TQ // d // A_BLOCK
        cur = pl.BlockSpec((1, d, TQ // d, A_GW), lambda b, j: (b, 0, j, 0))
        prev = pl.BlockSpec((1, d, A_BLOCK, A_GW), lambda b, j, nblk=nblk: (b, 0, jnp.maximum(j * nblk - 1, 0), 0))
        in_specs += [cur, cur, prev, cur, prev]
        args += [qs[g], ks[g], ks[g], vs[g], vs[g]]
    for _ in range(2):
        scratch += [pltpu.VMEM((npair, d, TQ // d, 128), F32) for _, d in A_GROUPS]
    return pl.pallas_call(
        _attn_fused_kernel,
        grid=(nbatch, nt),
        in_specs=in_specs,
        out_specs=pl.BlockSpec((npair, TQ, 128), lambda b, j: (0, b * nt + j, 0)),
        out_shape=jax.ShapeDtypeStruct((npair, nbatch * seq, 128), F32),
        scratch_shapes=scratch,
        compiler_params=_params(2),
        name="attn_prompt",
    )(*args)


NEW_PAD = 128


def _cached_bias(T, n_cached, window, dil):
    t = np.arange(A_HPG * T)[:, None] % T
    jd_c = n_cached + t - np.arange(n_cached)[None, :]
    jd_n = t - np.arange(NEW_PAD)[None, :]
    ok_c = (jd_c % dil == 0) & (jd_c <= window)
    ok_n = (jd_n >= 0) & (jd_n % dil == 0) & (jd_n <= window)
    to_bias = lambda ok: jnp.asarray(np.where(ok, 0.0, -np.inf), F32)
    return to_bias(ok_c), to_bias(ok_n)


def _attn_cached_kernel(q1_ref, q2_ref, q3_ref, kvn_ref, c1_ref, c2_ref, c3_ref,
                        bc1_ref, bc2_ref, bc3_ref, bn_ref, y_ref, *, T):
    rows = A_HPG * T
    q_refs, c_refs, bc_refs = (q1_ref, q2_ref, q3_ref), (c1_ref, c2_ref, c3_ref), (bc1_ref, bc2_ref, bc3_ref)
    row_head = lax.broadcasted_iota(jnp.int32, (rows, A_GW), 0) // T
    lane_head = lax.broadcasted_iota(jnp.int32, (rows, A_GW), 1) // A_HD
    out_head = lax.broadcasted_iota(jnp.int32, (T, A_GW), 1) // A_HD
    pad = jnp.zeros((NEW_PAD - T, A_GW), F32)
    batches = range(kvn_ref.shape[0])
    units = [(j, g) for j in batches for g in range(len(A_GROUPS))]
    uid = range(len(units))
    new_rows = lambda j, c0: jnp.concatenate([kvn_ref[j, :, c0:c0 + A_GW], pad], axis=0).astype(BF16)
    qbd = [jnp.where(row_head == lane_head, jnp.concatenate([q_refs[g][j]] * A_HPG, axis=0), 0.0).astype(BF16)
           for j, g in units]
    k_t = [c_refs[g][j, 0:A_GW, :].astype(BF16) for j, g in units]
    v_t = [c_refs[g][j, A_GW:2 * A_GW, :].astype(BF16) for j, g in units]
    kn = [new_rows(j, g * A_GW) for j, g in units]
    vn = [new_rows(j, A_WIDTH + g * A_GW) for j, g in units]
    s_c = [_dot(qbd[i], k_t[i]) + bc_refs[g][...] for i, (j, g) in enumerate(units)]
    s_n = [lax.dot_general(qbd[i], kn[i], NT_DIMS, preferred_element_type=F32) + bn_ref[g]
           for i, (j, g) in enumerate(units)]
    mx = [jnp.maximum(jnp.max(s_c[i], axis=-1, keepdims=True), jnp.max(s_n[i], axis=-1, keepdims=True)) for i in uid]
    e_c = [jnp.exp(s_c[i] - mx[i]) for i in uid]
    e_n = [jnp.exp(s_n[i] - mx[i]) for i in uid]
    den = [jnp.sum(e_c[i], axis=-1, keepdims=True) + jnp.sum(e_n[i], axis=-1, keepdims=True) for i in uid]
    res = [(lax.dot_general(e_c[i].astype(BF16), v_t[i], NT_DIMS, preferred_element_type=F32)
            + _dot(e_n[i].astype(BF16), vn[i])) / den[i] for i in uid]
    lse = [mx[i] + jnp.log(den[i]) for i in uid]
    o_l = []
    for i in uid:
        o = jnp.zeros((T, A_GW), F32)
        lb = jnp.zeros((T, A_GW), F32)
        for h in range(A_HPG):
            o = jnp.where(out_head == h, res[i][h * T:(h + 1) * T, :], o)
            lb = jnp.where(out_head == h, lse[i][h * T:(h + 1) * T, :], lb)
        o_l.append((o, lb))
    ng = len(A_GROUPS)
    for j in batches:
        y_ref[j] = _group_weights_combine(o_l[j * ng:(j + 1) * ng])


def _attn_cached(qs, kvn, caches, nb_step=2):
    nb, T, _ = kvn.shape
    per_b = lambda t: pl.BlockSpec((nb_step,) + t.shape[1:], lambda b: (b, 0, 0))
    biases = [_cached_bias(T, c.shape[2], w, d) for c, (w, d) in zip(caches, A_GROUPS)]
    bias_c = [bc for bc, _ in biases]
    bias_n = jnp.stack([bn for _, bn in biases])
    return pl.pallas_call(
        functools.partial(_attn_cached_kernel, T=T),
        grid=(nb // nb_step,),
        in_specs=[per_b(q) for q in qs] + [per_b(kvn)] + [per_b(c) for c in caches]
                 + [_resident(bc.shape) for bc in bias_c] + [_resident(bias_n.shape)],
        out_specs=pl.BlockSpec((nb_step, T, A_GW), lambda b: (b, 0, 0)),
        out_shape=jax.ShapeDtypeStruct((nb, T, A_GW), F32),
        compiler_params=_params(1),
        name="attn_cached",
    )(*qs, kvn, *caches, *bias_c, bias_n)


def _post_kernel(x_ref, hm_ref, ya_ref, gmix_ref, gmh_ref, wz_ref, wum_ref, wua_ref, wo_ref, out_ref):
    x = x_ref[...]
    h = _rms(x, gmix_ref[...]).astype(BF16)
    og = _dot(h, wz_ref[:, 0:M_WIDTH])
    gm = _dot(h, wz_ref[:, M_WIDTH:M_WIDTH + D_MODEL])
    ga = _dot(h, wz_ref[:, M_WIDTH + D_MODEL:M_WIDTH + 2 * D_MODEL])
    hm = hm_ref[...] * gmh_ref[...] * jax.nn.sigmoid(og)
    y_m = _dot(hm.astype(BF16), wum_ref[...])
    ya = jnp.concatenate([ya_ref[p] for p in range(ya_ref.shape[0])], axis=1)
    y_a = _dot(ya.astype(BF16), wua_ref[...])
    merged = jax.nn.sigmoid(gm) * y_m + jax.nn.sigmoid(ga) * y_a
    out_ref[...] = x + _dot(merged.astype(BF16), wo_ref[...])


def _post(x, hm, y_a, g_mix, g_mhead, w_z, w_up_m, w_up_a, w_out, tm=1024):
    n = x.shape[0]
    tm = min(tm, n)
    row = lambda w: pl.BlockSpec((tm, w), lambda i: (i, 0))
    return pl.pallas_call(
        _post_kernel,
        grid=(n // tm,),
        in_specs=[row(D_MODEL), row(M_WIDTH), pl.BlockSpec((y_a.shape[0], tm, 128), lambda i: (0, i, 0)),
                  _resident((1, D_MODEL)), _resident((1, M_WIDTH)), _resident(w_z.shape),
                  _resident(w_up_m.shape), _resident(w_up_a.shape), _resident(w_out.shape)],
        out_specs=row(D_MODEL),
        out_shape=jax.ShapeDtypeStruct((n, D_MODEL), F32),
        compiler_params=_params(1),
        name="post",
    )(x, hm, y_a, g_mix, g_mhead, w_z, w_up_m, w_up_a, w_out)


def _rope_tables(pos):
    half = A_HD // 2
    inv_freq = ROPE_THETA ** (-2.0 * jnp.arange(half, dtype=F32) / A_HD)
    ang = pos.astype(F32)[:, None] * inv_freq[None, :]
    cos = jnp.cos(ang)
    sin = jnp.sin(ang)
    return jnp.tile(cos, (1, 4)), jnp.tile(jnp.concatenate([-sin, sin], axis=1), (1, 2))


def _window_rows(kvf, nbatch, rows):
    res = []
    for g, r in enumerate(rows):
        kk = kvf[:, kvf.shape[1] - r:, g * A_GW:(g + 1) * A_GW].reshape(nbatch, r, A_HPG, A_HD)
        vv = kvf[:, kvf.shape[1] - r:, A_WIDTH + g * A_GW:A_WIDTH + (g + 1) * A_GW].reshape(nbatch, r, A_HPG, A_HD)
        res.append(jnp.stack([kk, vv], axis=2))
    return res


def kernel(x_prompt, x_sample, state_mlstm_C, state_mlstm_n, state_mlstm_m, cache_win128_kv, cache_win512_kv, cache_win2048_kv, g_ffn1, w1_gate, w1_up, w1_down, g_mix, w_in, b_igate, b_fgate, g_mhead, w_up_m, w_up_a, w_out, g_ffn2, w2_gate, w2_up, w2_down, g_final):
    B, S, _ = x_prompt.shape
    BS, T, _ = x_sample.shape
    depth = w_in.shape[0]
    caches_all = (cache_win128_kv, cache_win512_kv, cache_win2048_kv)

    cos_p, sin_p = _rope_tables(jnp.arange(S, dtype=jnp.int32))
    cos_s, sin_s = _rope_tables(PAST_LEN + jnp.arange(T, dtype=jnp.int32))
    cos_s = jnp.tile(cos_s, (BS, 1))
    sin_s = jnp.tile(sin_s, (BS, 1))

    xp = x_prompt.reshape(B * S, D_MODEL)
    xs = x_sample.reshape(BS * T, D_MODEL)
    L_P = 256
    L_S = 128
    NB_P, NB_S = 4, 8
    prompt_rows = tuple(min(w, S) for w, _ in A_GROUPS)
    tail = max(prompt_rows)
    dils = tuple(d for _, d in A_GROUPS)
    acc = [[] for _ in range(12)]
    for l in range(depth):
        bf = lambda w: w[l].astype(BF16)
        vec = lambda g: g[l].reshape(1, -1)
        wi = w_in[l]
        w_qkv = jnp.concatenate([wi[:, _C_QM:_C_IG], wi[:, _C_QA:_C_GM]], axis=1).astype(BF16)
        w_gate_t = wi[:, _C_IG:_C_OG].T.astype(BF16)
        w_z = jnp.concatenate([wi[:, _C_OG:_C_QA], wi[:, _C_GM:_C_END]], axis=1).astype(BF16)
        ffn1 = (vec(g_ffn1), bf(w1_gate), bf(w1_up), bf(w1_down))
        ffn2 = (vec(g_ffn2), bf(w2_gate), bf(w2_up), bf(w2_down))
        post_w = (vec(g_mix), vec(g_mhead), w_z, bf(w_up_m), bf(w_up_a), bf(w_out))
        gate_bias = jnp.concatenate([b_igate[l], b_fgate[l]]).reshape(8, 1)
        last = l == depth - 1
        gfin = g_final.reshape(1, -1) if last else None

        xp = _ffn(xp, *ffn1)
        qm, km, vm, *qkv_a, kvf, gt = _inproj(xp, vec(g_mix), w_qkv, w_gate_t, cos_p, sin_p, B, S, tail, dils)
        hm, c_p, n_p, m_p = _mlstm(
            qm.reshape(B, S, M_WIDTH), km.reshape(B, S, M_WIDTH), vm.reshape(B, S, M_WIDTH), gt, gate_bias,
            jnp.zeros((B, M_HEADS, M_DV, M_DK), F32), jnp.zeros((B, M_HEADS, M_DK), F32),
            jnp.zeros((B, M_HEADS, 1), F32), L_P, NB_P)
        y_a = _attn_prompt(qkv_a[0:3], qkv_a[3:6], qkv_a[6:9], B, S)
        xp = _post(xp, hm.reshape(B * S, M_WIDTH), y_a, *post_w)
        xp = _ffn(xp, *ffn2, g_final=gfin)
        kv_p = _window_rows(kvf, B, prompt_rows)
        for i, t in enumerate((c_p, n_p, m_p.reshape(B, M_HEADS), *kv_p)):
            acc[i].append(t)

        xs = _ffn(xs, *ffn1)
        qm, km, vm, *qkv_a, kvf, gt = _inproj(xs, vec(g_mix), w_qkv, w_gate_t, cos_s, sin_s, 1, BS * T, BS * T,
                                              (1,) * len(A_GROUPS))
        pad_tok = lambda t: jnp.pad(t.reshape(BS, T, M_WIDTH), ((0, 0), (0, L_S - T), (0, 0)))
        gt = gt.reshape(8, BS, T).transpose(1, 0, 2)
        gt = jnp.concatenate([jnp.pad(gt[:, 0:4], ((0, 0), (0, 0), (0, L_S - T)), constant_values=-1e30),
                              jnp.pad(gt[:, 4:8], ((0, 0), (0, 0), (0, L_S - T)), constant_values=1e30)], axis=1)
        hm, c_s, n_s, m_s = _mlstm(
            pad_tok(qm), pad_tok(km), pad_tok(vm), gt, gate_bias,
            state_mlstm_C[l], state_mlstm_n[l], state_mlstm_m[l].reshape(BS, M_HEADS, 1), L_S, NB_S)
        hm = hm[:, :T].reshape(BS * T, M_WIDTH)
        caches = [jnp.transpose(c[l], (0, 2, 3, 4, 1)).reshape(BS, 2 * A_GW, c.shape[2]) for c in caches_all]
        q_s = [q.astype(F32).reshape(BS, T, A_GW) for q in qkv_a[0:3]]
        y_a = _attn_cached(q_s, kvf.reshape(BS, T, 2 * A_WIDTH), caches)
        xs = _post(xs, hm, y_a.reshape(BS * T, A_GW // 128, 128).transpose(1, 0, 2), *post_w)
        xs = _ffn(xs, *ffn2, g_final=gfin)
        kv_s = _window_rows(kvf.reshape(BS, T, 2 * A_WIDTH), BS, (T,) * len(A_GROUPS))
        for i, t in enumerate((c_s, n_s, m_s.reshape(BS, M_HEADS), *kv_s)):
            acc[6 + i].append(t)

    stacked = [jnp.stack(a) for a in acc]
    return (xp.reshape(B, S, D_MODEL), xs.reshape(BS, T, D_MODEL), *stacked)
```

```python
import functools

import jax
import jax.numpy as jnp
import numpy as np
from jax import lax
from jax.experimental import pallas as pl
from jax.experimental.pallas import tpu as pltpu

F32 = jnp.float32
BF16 = jnp.bfloat16

D_MODEL = 1024
D_FF = 2816
PAST_LEN = 8192
M_HEADS = 4
M_DK = 128
M_DV = 128
M_WIDTH = M_HEADS * M_DV
A_GROUPS = ((128, 1), (512, 4), (2048, 16))
A_HPG = 4
A_HD = 64
A_GW = A_HPG * A_HD
A_WIDTH = len(A_GROUPS) * A_GW
A_BLOCK = 128
ROPE_THETA = 10000.0
EPS = 1e-6
NEG_INF = float("-inf")

_C_QM, _C_KM, _C_VM, _C_IG, _C_FG, _C_OG = 0, 512, 1024, 1536, 1540, 1544
_C_QA, _C_KA, _C_VA, _C_GM, _C_GA, _C_END = 2056, 2824, 3592, 4360, 5384, 6408

VMEM_LIMIT = 58 * 1024 * 1024
NT_DIMS = (((1,), (1,)), ((), ()))
TN_DIMS = (((0,), (0,)), ((), ()))


def _params(n_axes):
    return pltpu.CompilerParams(dimension_semantics=("arbitrary",) * n_axes,
                                vmem_limit_bytes=VMEM_LIMIT)


def _resident(shape):
    return pl.BlockSpec(shape, lambda *_: (0,) * len(shape), pipeline_mode=pl.Buffered(1))


def _rms(x, g):
    return x * lax.rsqrt(jnp.mean(x * x, axis=-1, keepdims=True) + EPS) * g


def _dot(a, b):
    return jnp.dot(a, b, preferred_element_type=F32)


FF_CHUNK = 256


def _ffn_kernel(x_ref, g_ref, wg_ref, wu_ref, wd_ref, *rest, final_norm):
    if final_norm:
        gf_ref, o_ref, act_ref = rest
    else:
        o_ref, act_ref = rest
    x = x_ref[...]
    h = _rms(x, g_ref[...]).astype(BF16)
    for c in range(D_FF // FF_CHUNK):
        sl = slice(c * FF_CHUNK, (c + 1) * FF_CHUNK)
        gate = _dot(h, wg_ref[:, sl])
        up = _dot(h, wu_ref[:, sl])
        act_ref[:, sl] = (gate * jax.nn.sigmoid(gate) * up).astype(BF16)
    out = x + 0.5 * _dot(act_ref[...], wd_ref[...])
    if final_norm:
        out = _rms(out, gf_ref[...])
    o_ref[...] = out


def _ffn(x, g, wg, wu, wd, g_final=None, tm=1024):
    n = x.shape[0]
    tm = min(tm, n)
    final_norm = g_final is not None
    row = pl.BlockSpec((tm, D_MODEL), lambda i: (i, 0))
    in_specs = [row, _resident((1, D_MODEL)), _resident((D_MODEL, D_FF)),
                _resident((D_MODEL, D_FF)), _resident((D_FF, D_MODEL))]
    args = [x, g, wg, wu, wd]
    if final_norm:
        in_specs.append(_resident((1, D_MODEL)))
        args.append(g_final)
    return pl.pallas_call(
        functools.partial(_ffn_kernel, final_norm=final_norm),
        grid=(n // tm,),
        in_specs=in_specs,
        out_specs=row,
        out_shape=jax.ShapeDtypeStruct((n, D_MODEL), F32),
        scratch_shapes=[pltpu.VMEM((tm, D_FF), BF16)],
        compiler_params=_params(1),
        name="ffn_final" if final_norm else "ffn",
    )(*args)


def _rope(t, cos, sin):
    lane = lax.broadcasted_iota(jnp.int32, cos.shape, 1)
    first_half = (lane % A_HD) < (A_HD // 2)
    outs = []
    for c in range(t.shape[1] // 128):
        tc = t[:, c * 128:(c + 1) * 128]
        partner = jnp.where(first_half, pltpu.roll(tc, 128 - A_HD // 2, 1), pltpu.roll(tc, A_HD // 2, 1))
        outs.append(tc * cos + partner * sin)
    return jnp.concatenate(outs, axis=1)


def _inproj_kernel(x_ref, g_ref, w_ref, wgt_ref, cos_ref, sin_ref,
                   qm_ref, km_ref, vm_ref, q1_ref, q2_ref, q3_ref, k1_ref, k2_ref, k3_ref,
                   v1_ref, v2_ref, v3_ref, kvf_ref, gt_ref, z_ref, *, dils):
    tm = x_ref.shape[0]

    def emit(z, refs):
        for c in range(A_WIDTH // 128):
            z_ref[c] = z[:, c * 128:(c + 1) * 128]
        per_group = A_GW // 128
        for g, (ref, d) in enumerate(zip(refs, dils)):
            for r in range(d):
                for p in range(per_group):
                    ref[0, r, :, p * 128:(p + 1) * 128] = z_ref[
                        g * per_group + p, pl.ds(r, tm // d, stride=d), :].astype(BF16)

    h = _rms(x_ref[...], g_ref[...]).astype(BF16)
    qm_ref[...] = _dot(h, w_ref[:, 0:512]).astype(BF16)
    km_ref[...] = (_dot(h, w_ref[:, 512:1024]) * (M_DK ** -0.5)).astype(BF16)
    vm_ref[...] = _dot(h, w_ref[:, 1024:1536]).astype(BF16)
    cos = cos_ref[...]
    sin = sin_ref[...]
    emit(_rope(_dot(h, w_ref[:, 1536:2304]), cos, sin) * (A_HD ** -0.5), (q1_ref, q2_ref, q3_ref))
    ka = _rope(_dot(h, w_ref[:, 2304:3072]), cos, sin)
    emit(ka, (k1_ref, k2_ref, k3_ref))
    va = _dot(h, w_ref[:, 3072:3840])
    emit(va, (v1_ref, v2_ref, v3_ref))
    kvf_ref[0, :, 0:A_WIDTH] = ka
    kvf_ref[0, :, A_WIDTH:2 * A_WIDTH] = va
    gt_ref[0] = lax.dot_general(wgt_ref[...], h, NT_DIMS, preferred_element_type=F32)


def _inproj(x, g, w_qkv, w_gate_t, cos, sin, nbatch, seq, tail, dils, tm=1024):
    n = x.shape[0]
    tm = min(tm, tail)
    tpb = seq // tm
    first_tail = tpb - tail // tm
    row = lambda w: pl.BlockSpec((tm, w), lambda i: (i, 0))
    table = pl.BlockSpec((tm, 128), lambda i: (i % tpb, 0))
    grouped_shape = [jax.ShapeDtypeStruct((nbatch, d, seq // d, A_GW), BF16) for d in dils]
    grouped_spec = [pl.BlockSpec((1, d, tm // d, A_GW), lambda i: (i // tpb, 0, i % tpb, 0)) for d in dils]
    out_shapes = (
        jax.ShapeDtypeStruct((n, 512), BF16), jax.ShapeDtypeStruct((n, 512), BF16),
        jax.ShapeDtypeStruct((n, 512), BF16),
        *grouped_shape, *grouped_shape, *grouped_shape,
        jax.ShapeDtypeStruct((nbatch, tail, 2 * A_WIDTH), F32),
        jax.ShapeDtypeStruct((nbatch, 8, seq), F32),
    )
    out_specs = (
        row(512), row(512), row(512), *grouped_spec, *grouped_spec, *grouped_spec,
        pl.BlockSpec((1, tm, 2 * A_WIDTH), lambda i: (i // tpb, jnp.maximum(i % tpb - first_tail, 0), 0)),
        pl.BlockSpec((1, 8, tm), lambda i: (i // tpb, 0, i % tpb)),
    )
    return pl.pallas_call(
        functools.partial(_inproj_kernel, dils=dils),
        grid=(n // tm,),
        in_specs=[row(D_MODEL), _resident((1, D_MODEL)), _resident(w_qkv.shape),
                  _resident(w_gate_t.shape), table, table],
        out_specs=out_specs,
        out_shape=out_shapes,
        scratch_shapes=[pltpu.VMEM((A_WIDTH // 128, tm, 128), F32)],
        compiler_params=_params(1),
        name="inproj",
    )(x, g, w_qkv, w_gate_t, cos, sin)


def _scan_lanes(xs, lane, op, identity):
    shift = 1
    while shift < xs[0].shape[1]:
        xs = [op(x, jnp.where(lane >= shift, pltpu.roll(x, shift, 1), identity)) for x in xs]
        shift *= 2
    return xs


def _mlstm_body(q_ref, k_ref, v_ref, gt_ref, bias_ref, c_ref, n_ref, m_ref, store_h, store_state, L):
    seqs = range(q_ref.shape[0])
    units = [(nb, h) for nb in seqs for h in range(M_HEADS)]
    uid = range(len(units))
    hs = lambda h: slice(h * M_DK, (h + 1) * M_DK)
    r_idx = lax.broadcasted_iota(jnp.int32, (L, L), 0)
    c_idx = lax.broadcasted_iota(jnp.int32, (L, L), 1)
    causal = c_idx <= r_idx
    lane = lax.broadcasted_iota(jnp.int32, (M_HEADS, L), 1)
    bias = bias_ref[...]

    gates = [gt_ref[nb] + bias for nb in seqs]
    ig = [g[0:4, :] for g in gates]
    lf = [jnp.minimum(g[4:8, :], 0.0) - jnp.log1p(jnp.exp(-jnp.abs(g[4:8, :]))) for g in gates]
    yield
    b = _scan_lanes(lf, lane, jnp.add, 0.0)
    yield
    a = [ig[nb] - b[nb] for nb in seqs]
    m_prev = [m_ref[nb] for nb in seqs]
    a_max = _scan_lanes(a, lane, jnp.maximum, NEG_INF)
    yield
    big_m = [jnp.maximum(m_prev[nb], a_max[nb]) for nb in seqs]
    m_t = [b[nb] + big_m[nb] for nb in seqs]
    w_inter = [jnp.exp(m_prev[nb] - big_m[nb]) for nb in seqs]
    e_negm = [jnp.exp(-m_t[nb]) for nb in seqs]
    b_last = [b[nb][:, L - 1:L] for nb in seqs]
    m_new = [m_t[nb][:, L - 1:L] for nb in seqs]
    decay = [jnp.exp(b_last[nb] + m_prev[nb] - m_new[nb]) for nb in seqs]
    w_s = [jnp.exp(a[nb] + (b_last[nb] - m_new[nb])) for nb in seqs]
    yield
    pad = jnp.zeros((128 - 4 * M_HEADS, L), F32)
    cols = [jnp.transpose(jnp.concatenate([big_m[nb], w_inter[nb], e_negm[nb], w_s[nb], pad], axis=0)) for nb in seqs]
    col = lambda j: [cols[nb][:, 4 * j + h:4 * j + h + 1] for nb, h in units]
    big_m_col, w_inter_col, e_negm_col, w_s_col = col(0), col(1), col(2), col(3)
    yield

    q = [q_ref[nb, :, hs(h)] for nb, h in units]
    k = [k_ref[nb, :, hs(h)] for nb, h in units]
    v = [v_ref[nb, :, hs(h)] for nb, h in units]
    c_old = [c_ref[nb, h] for nb, h in units]
    n_old = [n_ref[nb, h:h + 1, :] for nb, h in units]
    s = [lax.dot_general(q[i], k[i], NT_DIMS, preferred_element_type=F32) for i in uid]
    yield
    inter = [lax.dot_general(q[i], c_old[i].astype(BF16), NT_DIMS, preferred_element_type=F32) for i in uid]
    yield
    p = [s[i] * jnp.exp(jnp.where(causal, a[nb][h:h + 1, :] - big_m_col[i], NEG_INF))
         for i, (nb, h) in enumerate(units)]
    yield
    pv = [_dot(p[i].astype(BF16), v[i]) for i in uid]
    yield
    num = [w_inter_col[i] * inter[i] + pv[i] for i in uid]
    yield
    qn = [jnp.sum(q[i].astype(F32) * n_old[i], axis=-1, keepdims=True) for i in uid]
    yield
    den = [w_inter_col[i] * qn[i] + jnp.sum(p[i], axis=-1, keepdims=True) for i in uid]
    yield
    hh = [num[i] / jnp.maximum(jnp.abs(den[i]), e_negm_col[i]) for i in uid]
    yield
    hn = [hh[i] * lax.rsqrt(jnp.mean(hh[i] * hh[i], axis=-1, keepdims=True) + EPS) for i in uid]
    yield
    for i, (nb, h) in enumerate(units):
        store_h(nb, hs(h), hn[i])
    yield
    vw = [(v[i].astype(F32) * w_s_col[i]).astype(BF16) for i in uid]
    yield
    upd = [lax.dot_general(vw[i], k[i], TN_DIMS, preferred_element_type=F32) for i in uid]
    yield
    w_rows = [jnp.broadcast_to(w_s[nb][h:h + 1, :], (8, L)).astype(BF16) for nb, h in units]
    n_upd = [_dot(w_rows[i], k[i])[0:1, :] for i in uid]
    yield
    c_new = [decay[nb][h:h + 1, :] * c_old[i] + upd[i] for i, (nb, h) in enumerate(units)]
    n_new = [decay[nb][h:h + 1, :] * n_old[i] + n_upd[i] for i, (nb, h) in enumerate(units)]

    def write_state():
        for i, (nb, h) in enumerate(units):
            c_ref[nb, h] = c_new[i]
            n_ref[nb, h:h + 1, :] = n_new[i]
        for nb in seqs:
            m_ref[nb] = m_new[nb]
    store_state(write_state)


def _mlstm_kernel(q_ref, k_ref, v_ref, gt_ref, bias_ref, c0_ref, n0_ref, m0_ref,
                  h_ref, c_ref, n_ref, m_ref, *, L):
    @pl.when(pl.program_id(1) == 0)
    def _():
        c_ref[...] = c0_ref[...]
        n_ref[...] = n0_ref[...]
        m_ref[...] = m0_ref[...]

    def store_h(nb, cols, value):
        h_ref[nb, :, cols] = value

    for _ in _mlstm_body(q_ref, k_ref, v_ref, gt_ref, bias_ref, c_ref, n_ref, m_ref,
                         store_h, lambda write: write(), L):
        pass


def _mlstm(q, k, v, gates, bias, c0, n0, m0, L, nb_step):
    nb, seq, _ = q.shape
    nchunk = seq // L
    tok = pl.BlockSpec((nb_step, L, M_WIDTH), lambda b, c: (b, c, 0))
    st_c = pl.BlockSpec((nb_step, M_HEADS, M_DV, M_DK), lambda b, c: (b, 0, 0, 0))
    st_n = pl.BlockSpec((nb_step, M_HEADS, M_DK), lambda b, c: (b, 0, 0))
    st_m = pl.BlockSpec((nb_step, M_HEADS, 1), lambda b, c: (b, 0, 0))
    return pl.pallas_call(
        functools.partial(_mlstm_kernel, L=L),
        grid=(nb // nb_step, nchunk),
        in_specs=[tok, tok, tok, pl.BlockSpec((nb_step, 8, L), lambda b, c: (b, 0, c)),
                  pl.BlockSpec((8, 1), lambda b, c: (0, 0)), st_c, st_n, st_m],
        out_specs=(tok, st_c, st_n, st_m),
        out_shape=(jax.ShapeDtypeStruct((nb, seq, M_WIDTH), F32),
                   jax.ShapeDtypeStruct(c0.shape, F32), jax.ShapeDtypeStruct(n0.shape, F32),
                   jax.ShapeDtypeStruct(m0.shape, F32)),
        compiler_params=_params(2),
        name="mlstm",
    )(q, k, v, gates, bias, c0, n0, m0)


DIL_MAX = max(d for _, d in A_GROUPS)
TQ = A_BLOCK * DIL_MAX


def _group_weights_combine(o_l):
    mx = functools.reduce(jnp.maximum, [l for _, l in o_l])
    es = [jnp.exp(l - mx) for _, l in o_l]
    num = functools.reduce(jnp.add, [e * o for e, (o, _) in zip(es, o_l)])
    return num / functools.reduce(jnp.add, es)


def _attn_fused_kernel(*refs):
    ng = len(A_GROUPS)
    ins = [refs[5 * g:5 * g + 5] for g in range(ng)]
    y_ref = refs[5 * ng]
    o_scs = refs[5 * ng + 1:5 * ng + 1 + ng]
    l_scs = refs[5 * ng + 1 + ng:]
    not_first = pl.program_id(1) > 0
    q_t = lax.broadcasted_iota(jnp.int32, (2 * A_BLOCK, 2 * A_BLOCK), 0) % A_BLOCK
    key = lax.broadcasted_iota(jnp.int32, (2 * A_BLOCK, 2 * A_BLOCK), 1)
    band = jnp.logical_and(key >= q_t, key <= q_t + A_BLOCK)
    bias = jnp.where(band, 0.0, NEG_INF)
    bias_block0 = jnp.where(jnp.logical_and(band, jnp.logical_or(key >= A_BLOCK, not_first)), 0.0, NEG_INF)
    low = lax.broadcasted_iota(jnp.int32, (A_BLOCK, 2 * A_HD), 1) < A_HD
    batch_nt = (((2,), (2,)), ((0,), (0,)))
    batch_nn = (((2,), (1,)), ((0,), (0,)))

    def head_pairs(qp, ks, vs, mask_bias):
        zero = jnp.zeros_like(qp)
        q2 = jnp.concatenate([jnp.where(low, qp, zero), jnp.where(low, zero, qp)], axis=1)
        s = lax.dot_general(q2, ks, batch_nt, preferred_element_type=F32) + mask_bias
        mx = jnp.max(s, axis=-1, keepdims=True)
        e = jnp.exp(s - mx)
        den = jnp.sum(e, axis=-1, keepdims=True)
        o2 = lax.dot_general(e.astype(BF16), vs, batch_nn, preferred_element_type=F32) / den
        lse = mx + jnp.log(den)
        o = jnp.where(low, o2[:, 0:A_BLOCK], o2[:, A_BLOCK:])
        lb = jnp.where(low, jnp.broadcast_to(lse[:, 0:A_BLOCK], o.shape), jnp.broadcast_to(lse[:, A_BLOCK:], o.shape))
        return o, lb

    for g, (_, d) in enumerate(A_GROUPS):
        q_ref, k_ref, kp_ref, v_ref, vp_ref = ins[g]
        o_sc, l_sc = o_scs[g], l_scs[g]
        nblk = TQ // d // A_BLOCK
        rest = (nblk - 1) * A_BLOCK
        blocks = lambda t: t.reshape(d * (nblk - 1), A_BLOCK, 128)
        for p in range(A_GW // 128):
            cs = slice(p * 128, (p + 1) * 128)
            o, lb = head_pairs(
                q_ref[0, :, 0:A_BLOCK, cs],
                jnp.concatenate([kp_ref[0, :, :, cs], k_ref[0, :, 0:A_BLOCK, cs]], axis=1),
                jnp.concatenate([vp_ref[0, :, :, cs], v_ref[0, :, 0:A_BLOCK, cs]], axis=1), bias_block0)
            o_sc[p, :, 0:A_BLOCK, :] = o
            l_sc[p, :, 0:A_BLOCK, :] = lb
            if nblk > 1:
                o, lb = head_pairs(
                    blocks(q_ref[0, :, A_BLOCK:, cs]),
                    jnp.concatenate([blocks(k_ref[0, :, 0:rest, cs]), blocks(k_ref[0, :, A_BLOCK:, cs])], axis=1),
                    jnp.concatenate([blocks(v_ref[0, :, 0:rest, cs]), blocks(v_ref[0, :, A_BLOCK:, cs])], axis=1), bias)
                o_sc[p, :, A_BLOCK:, :] = o.reshape(d, rest, 128)
                l_sc[p, :, A_BLOCK:, :] = lb.reshape(d, rest, 128)

    for rho in range(DIL_MAX):
        for p in range(A_GW // 128):
            o_l = []
            for g, (_, d) in enumerate(A_GROUPS):
                rows = pl.ds(rho // d, A_BLOCK, stride=DIL_MAX // d) if d < DIL_MAX else slice(None)
                o_l.append((o_scs[g][p, rho % d, rows, :], l_scs[g][p, rho % d, rows, :]))
            y_ref[p, pl.ds(rho, A_BLOCK, stride=DIL_MAX), :] = _group_weights_combine(o_l)


def _attn_prompt(qs, ks, vs, nbatch, seq):
    nt = seq // TQ
    npair = A_GW // 128
    in_specs, args, scratch = [], [], []
    for g, (_, d) in enumerate(A_GROUPS):
        nblk = TQ // d // A_BLOCK
        cur = pl.BlockSpec((1, d, TQ // d, A_GW), lambda b, j: (b, 0, j, 0))
        prev = pl.BlockSpec((1, d, A_BLOCK, A_GW), lambda b, j, nblk=nblk: (b, 0, jnp.maximum(j * nblk - 1, 0), 0))
        in_specs += [cur, cur, prev, cur, prev]
        args += [qs[g], ks[g], ks[g], vs[g], vs[g]]
    for _ in range(2):
        scratch += [pltpu.VMEM((npair, d, TQ // d, 128), F32) for _, d in A_GROUPS]
    return pl.pallas_call(
        _attn_fused_kernel,
        grid=(nbatch, nt),
        in_specs=in_specs,
        out_specs=pl.BlockSpec((npair, TQ, 128), lambda b, j: (0, b * nt + j, 0)),
        out_shape=jax.ShapeDtypeStruct((npair, nbatch * seq, 128), F32),
        scratch_shapes=scratch,
        compiler_params=_params(2),
        name="attn_prompt",
    )(*args)


NEW_PAD = 128


def _cached_bias(T, n_cached, window, dil):
    t = np.arange(A_HPG * T)[:, None] % T
    jd_c = n_cached + t - np.arange(n_cached)[None, :]
    jd_n = t - np.arange(NEW_PAD)[None, :]
    ok_c = (jd_c % dil == 0) & (jd_c <= window)
    ok_n = (jd_n >= 0) & (jd_n % dil == 0) & (jd_n <= window)
    to_bias = lambda ok: jnp.asarray(np.where(ok, 0.0, -np.inf), F32)
    return to_bias(ok_c), to_bias(ok_n)


def _attn_cached_kernel(q1_ref, q2_ref, q3_ref, kvn_ref, c1_ref, c2_ref, c3_ref,
                        bc1_ref, bc2_ref, bc3_ref, bn_ref, y_ref, *, T):
    rows = A_HPG * T
    q_refs, c_refs, bc_refs = (q1_ref, q2_ref, q3_ref), (c1_ref, c2_ref, c3_ref), (bc1_ref, bc2_ref, bc3_ref)
    row_head = lax.broadcasted_iota(jnp.int32, (rows, A_GW), 0) // T
    lane_head = lax.broadcasted_iota(jnp.int32, (rows, A_GW), 1) // A_HD
    out_head = lax.broadcasted_iota(jnp.int32, (T, A_GW), 1) // A_HD
    pad = jnp.zeros((NEW_PAD - T, A_GW), F32)
    batches = range(kvn_ref.shape[0])
    units = [(j, g) for j in batches for g in range(len(A_GROUPS))]
    uid = range(len(units))
    new_rows = lambda j, c0: jnp.concatenate([kvn_ref[j, :, c0:c0 + A_GW], pad], axis=0).astype(BF16)
    qbd = [jnp.where(row_head == lane_head, jnp.concatenate([q_refs[g][j]] * A_HPG, axis=0), 0.0).astype(BF16)
           for j, g in units]
    k_t = [c_refs[g][j, 0:A_GW, :].astype(BF16) for j, g in units]
    v_t = [c_refs[g][j, A_GW:2 * A_GW, :].astype(BF16) for j, g in units]
    kn = [new_rows(j, g * A_GW) for j, g in units]
    vn = [new_rows(j, A_WIDTH + g * A_GW) for j, g in units]
    s_c = [_dot(qbd[i], k_t[i]) + bc_refs[g][...] for i, (j, g) in enumerate(units)]
    s_n = [lax.dot_general(qbd[i], kn[i], NT_DIMS, preferred_element_type=F32) + bn_ref[g]
           for i, (j, g) in enumerate(units)]
    mx = [jnp.maximum(jnp.max(s_c[i], axis=-1, keepdims=True), jnp.max(s_n[i], axis=-1, keepdims=True)) for i in uid]
    e_c = [jnp.exp(s_c[i] - mx[i]) for i in uid]
    e_n = [jnp.exp(s_n[i] - mx[i]) for i in uid]
    den = [jnp.sum(e_c[i], axis=-1, keepdims=True) + jnp.sum(e_n[i], axis=-1, keepdims=True) for i in uid]
    res = [(lax.dot_general(e_c[i].astype(BF16), v_t[i], NT_DIMS, preferred_element_type=F32)
            + _dot(e_n[i].astype(BF16), vn[i])) / den[i] for i in uid]
    lse = [mx[i] + jnp.log(den[i]) for i in uid]
    o_l = []
    for i in uid:
        o = jnp.zeros((T, A_GW), F32)
        lb = jnp.zeros((T, A_GW), F32)
        for h in range(A_HPG):
            o = jnp.where(out_head == h, res[i][h * T:(h + 1) * T, :], o)
            lb = jnp.where(out_head == h, lse[i][h * T:(h + 1) * T, :], lb)
        o_l.append((o, lb))
    ng = len(A_GROUPS)
    for j in batches:
        y_ref[j] = _group_weights_combine(o_l[j * ng:(j + 1) * ng])


def _attn_cached(qs, kvn, caches, nb_step=2):
    nb, T, _ = kvn.shape
    per_b = lambda t: pl.BlockSpec((nb_step,) + t.shape[1:], lambda b: (b, 0, 0))
    biases = [_cached_bias(T, c.shape[2], w, d) for c, (w, d) in zip(caches, A_GROUPS)]
    bias_c = [bc for bc, _ in biases]
    bias_n = jnp.stack([bn for _, bn in biases])
    return pl.pallas_call(
        functools.partial(_attn_cached_kernel, T=T),
        grid=(nb // nb_step,),
        in_specs=[per_b(q) for q in qs] + [per_b(kvn)] + [per_b(c) for c in caches]
                 + [_resident(bc.shape) for bc in bias_c] + [_resident(bias_n.shape)],
        out_specs=pl.BlockSpec((nb_step, T, A_GW), lambda b: (b, 0, 0)),
        out_shape=jax.ShapeDtypeStruct((nb, T, A_GW), F32),
        compiler_params=_params(1),
        name="attn_cached",
    )(*qs, kvn, *caches, *bias_c, bias_n)


POST_CHUNK = 256


def _post_body(get_x, hm, ya, gmix_ref, gmh_ref, wz_ref, wum_ref, wua_ref, wo_ref, store_out):
    cols = lambda c: slice(c * POST_CHUNK, (c + 1) * POST_CHUNK)
    h = _rms(get_x(slice(None)), gmix_ref[...]).astype(BF16)
    yield
    og = []
    for c in range(M_WIDTH // POST_CHUNK):
        og.append(_dot(h, wz_ref[:, cols(c)]))
        yield
    hm_act = (hm * gmh_ref[...] * jax.nn.sigmoid(jnp.concatenate(og, axis=1))).astype(BF16)
    ya_b = ya.astype(BF16)
    yield
    merged = []
    for c in range(D_MODEL // POST_CHUNK):
        gm = _dot(h, wz_ref[:, M_WIDTH + c * POST_CHUNK:M_WIDTH + (c + 1) * POST_CHUNK])
        yield
        ga = _dot(h, wz_ref[:, M_WIDTH + D_MODEL + c * POST_CHUNK:M_WIDTH + D_MODEL + (c + 1) * POST_CHUNK])
        yield
        y_m = _dot(hm_act, wum_ref[:, cols(c)])
        y_a = _dot(ya_b, wua_ref[:, cols(c)])
        yield
        merged.append((jax.nn.sigmoid(gm) * y_m + jax.nn.sigmoid(ga) * y_a).astype(BF16))
    merged = jnp.concatenate(merged, axis=1)
    for c in range(D_MODEL // POST_CHUNK):
        store_out(cols(c), get_x(cols(c)) + _dot(merged, wo_ref[:, cols(c)]))
        yield


def _post_kernel(x_ref, hm_ref, ya_ref, gmix_ref, gmh_ref, wz_ref, wum_ref, wua_ref, wo_ref, out_ref):
    ya = jnp.concatenate([ya_ref[p] for p in range(ya_ref.shape[0])], axis=1)

    def store_out(cols, value):
        out_ref[:, cols] = value

    for _ in _post_body(lambda cols: x_ref[:, cols], hm_ref[...], ya, gmix_ref, gmh_ref, wz_ref, wum_ref, wua_ref, wo_ref, store_out):
        pass


def _mixer_post_kernel(q_ref, k_ref, v_ref, gt_ref, bias_ref, c0_ref, n0_ref, m0_ref,
                       x_ref, ya_ref, gmix_ref, gmh_ref, wz_ref, wum_ref, wua_ref, wo_ref,
                       out_ref, c_ref, n_ref, m_ref, hm_ref, *, L, nchunk, nsteps):
    s = pl.program_id(0)
    cur = s % 2
    live = s < nsteps
    nb_step = q_ref.shape[0]
    rows = nb_step * L

    @pl.when(s == 0)
    def _():
        hm_ref[1] = jnp.zeros(hm_ref.shape[1:], F32)

    @pl.when(jnp.logical_and(live, s % nchunk == 0))
    def _():
        c_ref[...] = c0_ref[...]
        n_ref[...] = n0_ref[...]
        m_ref[...] = m0_ref[...]

    def store_h(nb, cols, value):
        hm_ref[cur, nb, :, cols] = value

    def store_out(cols, value):
        out_ref[:, :, cols] = value.reshape(nb_step, L, value.shape[1])

    get_x = lambda cols: x_ref[:, :, cols].reshape(rows, -1)
    hm = hm_ref[1 - cur].reshape(rows, M_WIDTH)
    ya = jnp.concatenate([ya_ref[p].reshape(rows, 128) for p in range(ya_ref.shape[0])], axis=1)
    mixer = _mlstm_body(q_ref, k_ref, v_ref, gt_ref, bias_ref, c_ref, n_ref, m_ref,
                        store_h, lambda write: pl.when(live)(write), L)
    post = _post_body(get_x, hm, ya, gmix_ref, gmh_ref, wz_ref, wum_ref, wua_ref, wo_ref, store_out)
    mixer_done = post_done = False
    while not (mixer_done and post_done):
        if not post_done:
            post_done = next(post, "done") == "done"
        if not mixer_done:
            mixer_done = next(mixer, "done") == "done"


def _mixer_post(q, k, v, gates, bias, c0, n0, m0, x, y_a, g_mix, g_mhead, w_z, w_up_m, w_up_a, w_out, L, nb_step):
    nb, seq, _ = q.shape
    nchunk = seq // L
    nsteps = (nb // nb_step) * nchunk
    cur = lambda s: (jnp.minimum(s, nsteps - 1) // nchunk, jnp.minimum(s, nsteps - 1) % nchunk)
    prv = lambda s: (jnp.maximum(s - 1, 0) // nchunk, jnp.maximum(s - 1, 0) % nchunk)
    tok = pl.BlockSpec((nb_step, L, M_WIDTH), lambda s: (*cur(s), 0))
    st_c = pl.BlockSpec((nb_step, M_HEADS, M_DV, M_DK), lambda s: (cur(s)[0], 0, 0, 0))
    st_n = pl.BlockSpec((nb_step, M_HEADS, M_DK), lambda s: (cur(s)[0], 0, 0))
    st_m = pl.BlockSpec((nb_step, M_HEADS, 1), lambda s: (cur(s)[0], 0, 0))
    x_spec = pl.BlockSpec((nb_step, L, D_MODEL), lambda s: (*prv(s), 0))
    return pl.pallas_call(
        functools.partial(_mixer_post_kernel, L=L, nchunk=nchunk, nsteps=nsteps),
        grid=(nsteps + 1,),
        in_specs=[tok, tok, tok, pl.BlockSpec((nb_step, 8, L), lambda s: (cur(s)[0], 0, cur(s)[1])),
                  _resident((8, 1)),
                  pl.BlockSpec(st_c.block_shape, st_c.index_map, pipeline_mode=pl.Buffered(1)), st_n, st_m,
                  x_spec, pl.BlockSpec((y_a.shape[0], nb_step, L, 128), lambda s: (0, *prv(s), 0)),
                  _resident((1, D_MODEL)), _resident((1, M_WIDTH)), _resident(w_z.shape),
                  _resident(w_up_m.shape), _resident(w_up_a.shape), _resident(w_out.shape)],
        out_specs=(x_spec, st_c, st_n, st_m),
        out_shape=(jax.ShapeDtypeStruct(x.shape, F32),
                   jax.ShapeDtypeStruct(c0.shape, F32), jax.ShapeDtypeStruct(n0.shape, F32),
                   jax.ShapeDtypeStruct(m0.shape, F32)),
        scratch_shapes=[pltpu.VMEM((2, nb_step, L, M_WIDTH), F32)],
        compiler_params=_params(1),
        name="mixer_post",
    )(q, k, v, gates, bias, c0, n0, m0, x, y_a, g_mix, g_mhead, w_z, w_up_m, w_up_a, w_out)


def _post(x, hm, y_a, g_mix, g_mhead, w_z, w_up_m, w_up_a, w_out, tm=1024):
    n = x.shape[0]
    tm = min(tm, n)
    row = lambda w: pl.BlockSpec((tm, w), lambda i: (i, 0))
    return pl.pallas_call(
        _post_kernel,
        grid=(n // tm,),
        in_specs=[row(D_MODEL), row(M_WIDTH), pl.BlockSpec((y_a.shape[0], tm, 128), lambda i: (0, i, 0)),
                  _resident((1, D_MODEL)), _resident((1, M_WIDTH)), _resident(w_z.shape),
                  _resident(w_up_m.shape), _resident(w_up_a.shape), _resident(w_out.shape)],
        out_specs=row(D_MODEL),
        out_shape=jax.ShapeDtypeStruct((n, D_MODEL), F32),
        compiler_params=_params(1),
        name="post",
    )(x, hm, y_a, g_mix, g_mhead, w_z, w_up_m, w_up_a, w_out)


def _rope_tables(pos):
    half = A_HD // 2
    inv_freq = ROPE_THETA ** (-2.0 * jnp.arange(half, dtype=F32) / A_HD)
    ang = pos.astype(F32)[:, None] * inv_freq[None, :]
    cos = jnp.cos(ang)
    sin = jnp.sin(ang)
    return jnp.tile(cos, (1, 4)), jnp.tile(jnp.concatenate([-sin, sin], axis=1), (1, 2))


def _window_rows(kvf, nbatch, rows):
    res = []
    for g, r in enumerate(rows):
        kk = kvf[:, kvf.shape[1] - r:, g * A_GW:(g + 1) * A_GW].reshape(nbatch, r, A_HPG, A_HD)
        vv = kvf[:, kvf.shape[1] - r:, A_WIDTH + g * A_GW:A_WIDTH + (g + 1) * A_GW].reshape(nbatch, r, A_HPG, A_HD)
        res.append(jnp.stack([kk, vv], axis=2))
    return res


def kernel(x_prompt, x_sample, state_mlstm_C, state_mlstm_n, state_mlstm_m, cache_win128_kv, cache_win512_kv, cache_win2048_kv, g_ffn1, w1_gate, w1_up, w1_down, g_mix, w_in, b_igate, b_fgate, g_mhead, w_up_m, w_up_a, w_out, g_ffn2, w2_gate, w2_up, w2_down, g_final):
    B, S, _ = x_prompt.shape
    BS, T, _ = x_sample.shape
    depth = w_in.shape[0]
    caches_all = (cache_win128_kv, cache_win512_kv, cache_win2048_kv)

    cos_p, sin_p = _rope_tables(jnp.arange(S, dtype=jnp.int32))
    cos_s, sin_s = _rope_tables(PAST_LEN + jnp.arange(T, dtype=jnp.int32))
    cos_s = jnp.tile(cos_s, (BS, 1))
    sin_s = jnp.tile(sin_s, (BS, 1))

    xp = x_prompt.reshape(B * S, D_MODEL)
    xs = x_sample.reshape(BS * T, D_MODEL)
    L_P = 256
    L_S = 128
    NB_P, NB_S = 4, 8
    prompt_rows = tuple(min(w, S) for w, _ in A_GROUPS)
    tail = max(prompt_rows)
    dils = tuple(d for _, d in A_GROUPS)
    acc = [[] for _ in range(12)]
    for l in range(depth):
        bf = lambda w: w[l].astype(BF16)
        vec = lambda g: g[l].reshape(1, -1)
        wi = w_in[l]
        w_qkv = jnp.concatenate([wi[:, _C_QM:_C_IG], wi[:, _C_QA:_C_GM]], axis=1).astype(BF16)
        w_gate_t = wi[:, _C_IG:_C_OG].T.astype(BF16)
        w_z = jnp.concatenate([wi[:, _C_OG:_C_QA], wi[:, _C_GM:_C_END]], axis=1).astype(BF16)
        ffn1 = (vec(g_ffn1), bf(w1_gate), bf(w1_up), bf(w1_down))
        ffn2 = (vec(g_ffn2), bf(w2_gate), bf(w2_up), bf(w2_down))
        post_w = (vec(g_mix), vec(g_mhead), w_z, bf(w_up_m), bf(w_up_a), bf(w_out))
        gate_bias = jnp.concatenate([b_igate[l], b_fgate[l]]).reshape(8, 1)
        last = l == depth - 1
        gfin = g_final.reshape(1, -1) if last else None

        xp = _ffn(xp, *ffn1)
        qm, km, vm, *qkv_a, kvf, gt = _inproj(xp, vec(g_mix), w_qkv, w_gate_t, cos_p, sin_p, B, S, tail, dils)
        y_a = _attn_prompt(qkv_a[0:3], qkv_a[3:6], qkv_a[6:9], B, S)
        xp, c_p, n_p, m_p = _mixer_post(
            qm.reshape(B, S, M_WIDTH), km.reshape(B, S, M_WIDTH), vm.reshape(B, S, M_WIDTH), gt, gate_bias,
            jnp.zeros((B, M_HEADS, M_DV, M_DK), F32), jnp.zeros((B, M_HEADS, M_DK), F32),
            jnp.zeros((B, M_HEADS, 1), F32), xp.reshape(B, S, D_MODEL), y_a.reshape(y_a.shape[0], B, S, 128),
            *post_w, L_P, NB_P)
        xp = xp.reshape(B * S, D_MODEL)
        xp = _ffn(xp, *ffn2, g_final=gfin)
        kv_p = _window_rows(kvf, B, prompt_rows)
        for i, t in enumerate((c_p, n_p, m_p.reshape(B, M_HEADS), *kv_p)):
            acc[i].append(t)

        xs = _ffn(xs, *ffn1)
        qm, km, vm, *qkv_a, kvf, gt = _inproj(xs, vec(g_mix), w_qkv, w_gate_t, cos_s, sin_s, 1, BS * T, BS * T,
                                              (1,) * len(A_GROUPS))
        pad_tok = lambda t: jnp.pad(t.reshape(BS, T, M_WIDTH), ((0, 0), (0, L_S - T), (0, 0)))
        gt = gt.reshape(8, BS, T).transpose(1, 0, 2)
        gt = jnp.concatenate([jnp.pad(gt[:, 0:4], ((0, 0), (0, 0), (0, L_S - T)), constant_values=-1e30),
                              jnp.pad(gt[:, 4:8], ((0, 0), (0, 0), (0, L_S - T)), constant_values=1e30)], axis=1)
        hm, c_s, n_s, m_s = _mlstm(
            pad_tok(qm), pad_tok(km), pad_tok(vm), gt, gate_bias,
            state_mlstm_C[l], state_mlstm_n[l], state_mlstm_m[l].reshape(BS, M_HEADS, 1), L_S, NB_S)
        hm = hm[:, :T].reshape(BS * T, M_WIDTH)
        caches = [jnp.transpose(c[l], (0, 2, 3, 4, 1)).reshape(BS, 2 * A_GW, c.shape[2]) for c in caches_all]
        q_s = [q.astype(F32).reshape(BS, T, A_GW) for q in qkv_a[0:3]]
        y_a = _attn_cached(q_s, kvf.reshape(BS, T, 2 * A_WIDTH), caches)
        xs = _post(xs, hm, y_a.reshape(BS * T, A_GW // 128, 128).transpose(1, 0, 2), *post_w)
        xs = _ffn(xs, *ffn2, g_final=gfin)
        kv_s = _window_rows(kvf.reshape(BS, T, 2 * A_WIDTH), BS, (T,) * len(A_GROUPS))
        for i, t in enumerate((c_s, n_s, m_s.reshape(BS, M_HEADS), *kv_s)):
            acc[6 + i].append(t)

    stacked = [jnp.stack(a) for a in acc]
    return (xp.reshape(B, S, D_MODEL), xs.reshape(BS, T, D_MODEL), *stacked)
```

```python
import functools

import jax
import jax.numpy as jnp
import numpy as np
from jax import lax
from jax.experimental import pallas as pl
from jax.experimental.pallas import tpu as pltpu

F32 = jnp.float32
BF16 = jnp.bfloat16

D_MODEL = 1024
D_FF = 2816
PAST_LEN = 8192
M_HEADS = 4
M_DK = 128
M_DV = 128
M_WIDTH = M_HEADS * M_DV
A_GROUPS = ((128, 1), (512, 4), (2048, 16))
A_HPG = 4
A_HD = 64
A_GW = A_HPG * A_HD
A_WIDTH = len(A_GROUPS) * A_GW
A_BLOCK = 128
ROPE_THETA = 10000.0
EPS = 1e-6
NEG_INF = float("-inf")

_C_QM, _C_KM, _C_VM, _C_IG, _C_FG, _C_OG = 0, 512, 1024, 1536, 1540, 1544
_C_QA, _C_KA, _C_VA, _C_GM, _C_GA, _C_END = 2056, 2824, 3592, 4360, 5384, 6408

VMEM_LIMIT = 58 * 1024 * 1024
NT_DIMS = (((1,), (1,)), ((), ()))
TN_DIMS = (((0,), (0,)), ((), ()))


def _params(n_axes):
    return pltpu.CompilerParams(dimension_semantics=("arbitrary",) * n_axes,
                                vmem_limit_bytes=VMEM_LIMIT)


def _resident(shape):
    return pl.BlockSpec(shape, lambda *_: (0,) * len(shape), pipeline_mode=pl.Buffered(1))


def _rms(x, g):
    return x * lax.rsqrt(jnp.mean(x * x, axis=-1, keepdims=True) + EPS) * g


def _dot(a, b):
    return jnp.dot(a, b, preferred_element_type=F32)


FF_CHUNK = 256


def _interleave(main, rider, rider_per_main):
    main_done = rider_done = False
    while not (main_done and rider_done):
        if not main_done:
            main_done = next(main, "done") == "done"
        for _ in range(rider_per_main):
            if not rider_done:
                rider_done = next(rider, "done") == "done"


def _ffn_body(x_ref, g_ref, wg_ref, wu_ref, wd_ref, gf_ref, o_ref, act_ref):
    x = x_ref[...]
    h = _rms(x, g_ref[...]).astype(BF16)
    yield
    for c in range(D_FF // FF_CHUNK):
        sl = slice(c * FF_CHUNK, (c + 1) * FF_CHUNK)
        gate = _dot(h, wg_ref[:, sl])
        up = _dot(h, wu_ref[:, sl])
        act_ref[:, sl] = (gate * jax.nn.sigmoid(gate) * up).astype(BF16)
        yield
    out = x + 0.5 * _dot(act_ref[...], wd_ref[...])
    if gf_ref is not None:
        out = _rms(out, gf_ref[...])
    o_ref[...] = out


def _ffn_kernel(*refs, final_norm, rider):
    n_in = 6 if final_norm else 5
    x_ref, g_ref, wg_ref, wu_ref, wd_ref = refs[:5]
    gf_ref = refs[5] if final_norm else None
    n_rin, n_rout = (rider.n_in, rider.n_out) if rider else (0, 0)
    rider_in = refs[n_in:n_in + n_rin]
    o_ref = refs[n_in + n_rin]
    rider_out = refs[n_in + n_rin + 1:n_in + n_rin + 1 + n_rout]
    act_ref = refs[-1]
    main = _ffn_body(x_ref, g_ref, wg_ref, wu_ref, wd_ref, gf_ref, o_ref, act_ref)
    if rider:
        _interleave(main, rider.body(rider_in, rider_out), rider.per_main)
    else:
        for _ in main:
            pass


class _Rider:
    def __init__(self, name, body, args, in_specs, out_specs, out_shapes, per_main):
        self.name, self.body, self.args, self.per_main = name, body, args, per_main
        self.in_specs, self.out_specs, self.out_shapes = in_specs, out_specs, out_shapes
        self.n_in, self.n_out = len(in_specs), len(out_specs)


def _ffn(x, g, wg, wu, wd, g_final=None, tm=1024, rider=None):
    n = x.shape[0]
    tm = min(tm, n)
    final_norm = g_final is not None
    row = pl.BlockSpec((tm, D_MODEL), lambda i: (i, 0))
    in_specs = [row, _resident((1, D_MODEL)), _resident((D_MODEL, D_FF)),
                _resident((D_MODEL, D_FF)), _resident((D_FF, D_MODEL))]
    args = [x, g, wg, wu, wd]
    if final_norm:
        in_specs.append(_resident((1, D_MODEL)))
        args.append(g_final)
    out_specs, out_shapes = [row], [jax.ShapeDtypeStruct((n, D_MODEL), F32)]
    name = "ffn_final" if final_norm else "ffn"
    if rider:
        in_specs += rider.in_specs
        args += rider.args
        out_specs += rider.out_specs
        out_shapes += rider.out_shapes
        name += "_" + rider.name
    res = pl.pallas_call(
        functools.partial(_ffn_kernel, final_norm=final_norm, rider=rider),
        grid=(n // tm,),
        in_specs=in_specs,
        out_specs=tuple(out_specs),
        out_shape=tuple(out_shapes),
        scratch_shapes=[pltpu.VMEM((tm, D_FF), BF16)],
        compiler_params=_params(1),
        name=name,
    )(*args)
    return (res[0], res[1:]) if rider else res[0]


def _rope(t, cos, sin):
    lane = lax.broadcasted_iota(jnp.int32, cos.shape, 1)
    first_half = (lane % A_HD) < (A_HD // 2)
    outs = []
    for c in range(t.shape[1] // 128):
        tc = t[:, c * 128:(c + 1) * 128]
        partner = jnp.where(first_half, pltpu.roll(tc, 128 - A_HD // 2, 1), pltpu.roll(tc, A_HD // 2, 1))
        outs.append(tc * cos + partner * sin)
    return jnp.concatenate(outs, axis=1)


def _inproj_kernel(x_ref, g_ref, w_ref, wgt_ref, cos_ref, sin_ref,
                   qm_ref, km_ref, vm_ref, q1_ref, q2_ref, q3_ref, k1_ref, k2_ref, k3_ref,
                   v1_ref, v2_ref, v3_ref, kvf_ref, gt_ref, z_ref, *, dils):
    tm = x_ref.shape[0]

    def emit(z, refs):
        for c in range(A_WIDTH // 128):
            z_ref[c] = z[:, c * 128:(c + 1) * 128]
        per_group = A_GW // 128
        for g, (ref, d) in enumerate(zip(refs, dils)):
            for r in range(d):
                for p in range(per_group):
                    ref[0, r, :, p * 128:(p + 1) * 128] = z_ref[
                        g * per_group + p, pl.ds(r, tm // d, stride=d), :].astype(BF16)

    h = _rms(x_ref[...], g_ref[...]).astype(BF16)
    qm_ref[...] = _dot(h, w_ref[:, 0:512]).astype(BF16)
    km_ref[...] = (_dot(h, w_ref[:, 512:1024]) * (M_DK ** -0.5)).astype(BF16)
    vm_ref[...] = _dot(h, w_ref[:, 1024:1536]).astype(BF16)
    cos = cos_ref[...]
    sin = sin_ref[...]
    emit(_rope(_dot(h, w_ref[:, 1536:2304]), cos, sin) * (A_HD ** -0.5), (q1_ref, q2_ref, q3_ref))
    ka = _rope(_dot(h, w_ref[:, 2304:3072]), cos, sin)
    emit(ka, (k1_ref, k2_ref, k3_ref))
    va = _dot(h, w_ref[:, 3072:3840])
    emit(va, (v1_ref, v2_ref, v3_ref))
    kvf_ref[0, :, 0:A_WIDTH] = ka
    kvf_ref[0, :, A_WIDTH:2 * A_WIDTH] = va
    gt_ref[0] = lax.dot_general(wgt_ref[...], h, NT_DIMS, preferred_element_type=F32)


def _inproj(x, g, w_qkv, w_gate_t, cos, sin, nbatch, seq, tail, dils, tm=1024):
    n = x.shape[0]
    tm = min(tm, tail)
    tpb = seq // tm
    first_tail = tpb - tail // tm
    row = lambda w: pl.BlockSpec((tm, w), lambda i: (i, 0))
    table = pl.BlockSpec((tm, 128), lambda i: (i % tpb, 0))
    grouped_shape = [jax.ShapeDtypeStruct((nbatch, d, seq // d, A_GW), BF16) for d in dils]
    grouped_spec = [pl.BlockSpec((1, d, tm // d, A_GW), lambda i: (i // tpb, 0, i % tpb, 0)) for d in dils]
    out_shapes = (
        jax.ShapeDtypeStruct((n, 512), BF16), jax.ShapeDtypeStruct((n, 512), BF16),
        jax.ShapeDtypeStruct((n, 512), BF16),
        *grouped_shape, *grouped_shape, *grouped_shape,
        jax.ShapeDtypeStruct((nbatch, tail, 2 * A_WIDTH), F32),
        jax.ShapeDtypeStruct((nbatch, 8, seq), F32),
    )
    out_specs = (
        row(512), row(512), row(512), *grouped_spec, *grouped_spec, *grouped_spec,
        pl.BlockSpec((1, tm, 2 * A_WIDTH), lambda i: (i // tpb, jnp.maximum(i % tpb - first_tail, 0), 0)),
        pl.BlockSpec((1, 8, tm), lambda i: (i // tpb, 0, i % tpb)),
    )
    return pl.pallas_call(
        functools.partial(_inproj_kernel, dils=dils),
        grid=(n // tm,),
        in_specs=[row(D_MODEL), _resident((1, D_MODEL)), _resident(w_qkv.shape),
                  _resident(w_gate_t.shape), table, table],
        out_specs=out_specs,
        out_shape=out_shapes,
        scratch_shapes=[pltpu.VMEM((A_WIDTH // 128, tm, 128), F32)],
        compiler_params=_params(1),
        name="inproj",
    )(x, g, w_qkv, w_gate_t, cos, sin)


def _scan_lanes(xs, lane, op, identity):
    shift = 1
    while shift < xs[0].shape[1]:
        xs = [op(x, jnp.where(lane >= shift, pltpu.roll(x, shift, 1), identity)) for x in xs]
        shift *= 2
    return xs


def _mlstm_body(q_ref, k_ref, v_ref, gt_ref, bias_ref, c_ref, n_ref, m_ref, store_h, store_state, L):
    seqs = range(q_ref.shape[0])
    units = [(nb, h) for nb in seqs for h in range(M_HEADS)]
    uid = range(len(units))
    hs = lambda h: slice(h * M_DK, (h + 1) * M_DK)
    r_idx = lax.broadcasted_iota(jnp.int32, (L, L), 0)
    c_idx = lax.broadcasted_iota(jnp.int32, (L, L), 1)
    causal = c_idx <= r_idx
    lane = lax.broadcasted_iota(jnp.int32, (M_HEADS, L), 1)
    bias = bias_ref[...]

    gates = [gt_ref[nb] + bias for nb in seqs]
    ig = [g[0:4, :] for g in gates]
    lf = [jnp.minimum(g[4:8, :], 0.0) - jnp.log1p(jnp.exp(-jnp.abs(g[4:8, :]))) for g in gates]
    yield
    b = _scan_lanes(lf, lane, jnp.add, 0.0)
    yield
    a = [ig[nb] - b[nb] for nb in seqs]
    m_prev = [m_ref[nb] for nb in seqs]
    a_max = _scan_lanes(a, lane, jnp.maximum, NEG_INF)
    yield
    big_m = [jnp.maximum(m_prev[nb], a_max[nb]) for nb in seqs]
    m_t = [b[nb] + big_m[nb] for nb in seqs]
    w_inter = [jnp.exp(m_prev[nb] - big_m[nb]) for nb in seqs]
    e_negm = [jnp.exp(-m_t[nb]) for nb in seqs]
    b_last = [b[nb][:, L - 1:L] for nb in seqs]
    m_new = [m_t[nb][:, L - 1:L] for nb in seqs]
    decay = [jnp.exp(b_last[nb] + m_prev[nb] - m_new[nb]) for nb in seqs]
    w_s = [jnp.exp(a[nb] + (b_last[nb] - m_new[nb])) for nb in seqs]
    yield
    pad = jnp.zeros((128 - 4 * M_HEADS, L), F32)
    cols = [jnp.transpose(jnp.concatenate([big_m[nb], w_inter[nb], e_negm[nb], w_s[nb], pad], axis=0)) for nb in seqs]
    col = lambda j: [cols[nb][:, 4 * j + h:4 * j + h + 1] for nb, h in units]
    big_m_col, w_inter_col, e_negm_col, w_s_col = col(0), col(1), col(2), col(3)
    yield

    q = [q_ref[nb, :, hs(h)] for nb, h in units]
    k = [k_ref[nb, :, hs(h)] for nb, h in units]
    v = [v_ref[nb, :, hs(h)] for nb, h in units]
    c_old = [c_ref[nb, h] for nb, h in units]
    n_old = [n_ref[nb, h:h + 1, :] for nb, h in units]
    s = [lax.dot_general(q[i], k[i], NT_DIMS, preferred_element_type=F32) for i in uid]
    yield
    inter = [lax.dot_general(q[i], c_old[i].astype(BF16), NT_DIMS, preferred_element_type=F32) for i in uid]
    yield
    p = [s[i] * jnp.exp(jnp.where(causal, a[nb][h:h + 1, :] - big_m_col[i], NEG_INF))
         for i, (nb, h) in enumerate(units)]
    yield
    pv = [_dot(p[i].astype(BF16), v[i]) for i in uid]
    yield
    num = [w_inter_col[i] * inter[i] + pv[i] for i in uid]
    yield
    qn = [jnp.sum(q[i].astype(F32) * n_old[i], axis=-1, keepdims=True) for i in uid]
    yield
    den = [w_inter_col[i] * qn[i] + jnp.sum(p[i], axis=-1, keepdims=True) for i in uid]
    yield
    hh = [num[i] / jnp.maximum(jnp.abs(den[i]), e_negm_col[i]) for i in uid]
    yield
    hn = [hh[i] * lax.rsqrt(jnp.mean(hh[i] * hh[i], axis=-1, keepdims=True) + EPS) for i in uid]
    yield
    for i, (nb, h) in enumerate(units):
        store_h(nb, hs(h), hn[i])
    yield
    vw = [(v[i].astype(F32) * w_s_col[i]).astype(BF16) for i in uid]
    yield
    upd = [lax.dot_general(vw[i], k[i], TN_DIMS, preferred_element_type=F32) for i in uid]
    yield
    w_rows = [jnp.broadcast_to(w_s[nb][h:h + 1, :], (8, L)).astype(BF16) for nb, h in units]
    n_upd = [_dot(w_rows[i], k[i])[0:1, :] for i in uid]
    yield
    c_new = [decay[nb][h:h + 1, :] * c_old[i] + upd[i] for i, (nb, h) in enumerate(units)]
    n_new = [decay[nb][h:h + 1, :] * n_old[i] + n_upd[i] for i, (nb, h) in enumerate(units)]

    def write_state():
        for i, (nb, h) in enumerate(units):
            c_ref[nb, h] = c_new[i]
            n_ref[nb, h:h + 1, :] = n_new[i]
        for nb in seqs:
            m_ref[nb] = m_new[nb]
    store_state(write_state)


def _mlstm_rider(q, k, v, gates, bias, c0, n0, m0, n_steps):
    nb, L, _ = q.shape
    nb_step = nb // n_steps
    blk = lambda t: pl.BlockSpec((nb_step,) + t.shape[1:], lambda i: (i,) + (0,) * (t.ndim - 1))

    def body(in_refs, out_refs):
        q_ref, k_ref, v_ref, gt_ref, bias_ref, c0_ref, n0_ref, m0_ref = in_refs
        h_ref, c_ref, n_ref, m_ref = out_refs
        c_ref[...] = c0_ref[...]
        n_ref[...] = n0_ref[...]
        m_ref[...] = m0_ref[...]

        def store_h(nb, cols, value):
            h_ref[nb, :, cols] = value

        yield from _mlstm_body(q_ref, k_ref, v_ref, gt_ref, bias_ref, c_ref, n_ref, m_ref,
                               store_h, lambda write: write(), L)

    outs = (jax.ShapeDtypeStruct((nb, L, M_WIDTH), F32), jax.ShapeDtypeStruct(c0.shape, F32),
            jax.ShapeDtypeStruct(n0.shape, F32), jax.ShapeDtypeStruct(m0.shape, F32))
    return _Rider("mlstm", body, [q, k, v, gates, bias, c0, n0, m0],
                  [blk(q), blk(k), blk(v), blk(gates), _resident(bias.shape), blk(c0), blk(n0), blk(m0)],
                  [blk(o) for o in outs], list(outs), per_main=2)


DIL_MAX = max(d for _, d in A_GROUPS)
TQ = A_BLOCK * DIL_MAX


def _group_weights_combine(o_l):
    mx = functools.reduce(jnp.maximum, [l for _, l in o_l])
    es = [jnp.exp(l - mx) for _, l in o_l]
    num = functools.reduce(jnp.add, [e * o for e, (o, _) in zip(es, o_l)])
    return num / functools.reduce(jnp.add, es)


def _attn_fused_kernel(*refs):
    ng = len(A_GROUPS)
    ins = [refs[5 * g:5 * g + 5] for g in range(ng)]
    y_ref = refs[5 * ng]
    o_scs = refs[5 * ng + 1:5 * ng + 1 + ng]
    l_scs = refs[5 * ng + 1 + ng:]
    not_first = pl.program_id(1) > 0
    q_t = lax.broadcasted_iota(jnp.int32, (2 * A_BLOCK, 2 * A_BLOCK), 0) % A_BLOCK
    key = lax.broadcasted_iota(jnp.int32, (2 * A_BLOCK, 2 * A_BLOCK), 1)
    band = jnp.logical_and(key >= q_t, key <= q_t + A_BLOCK)
    bias = jnp.where(band, 0.0, NEG_INF)
    bias_block0 = jnp.where(jnp.logical_and(band, jnp.logical_or(key >= A_BLOCK, not_first)), 0.0, NEG_INF)
    low = lax.broadcasted_iota(jnp.int32, (A_BLOCK, 2 * A_HD), 1) < A_HD
    batch_nt = (((2,), (2,)), ((0,), (0,)))
    batch_nn = (((2,), (1,)), ((0,), (0,)))

    def head_pairs(qp, ks, vs, mask_bias):
        zero = jnp.zeros_like(qp)
        q2 = jnp.concatenate([jnp.where(low, qp, zero), jnp.where(low, zero, qp)], axis=1)
        s = lax.dot_general(q2, ks, batch_nt, preferred_element_type=F32) + mask_bias
        mx = jnp.max(s, axis=-1, keepdims=True)
        e = jnp.exp(s - mx)
        den = jnp.sum(e, axis=-1, keepdims=True)
        o2 = lax.dot_general(e.astype(BF16), vs, batch_nn, preferred_element_type=F32) / den
        lse = mx + jnp.log(den)
        o = jnp.where(low, o2[:, 0:A_BLOCK], o2[:, A_BLOCK:])
        lb = jnp.where(low, jnp.broadcast_to(lse[:, 0:A_BLOCK], o.shape), jnp.broadcast_to(lse[:, A_BLOCK:], o.shape))
        return o, lb

    for g, (_, d) in enumerate(A_GROUPS):
        q_ref, k_ref, kp_ref, v_ref, vp_ref = ins[g]
        o_sc, l_sc = o_scs[g], l_scs[g]
        nblk = TQ // d // A_BLOCK
        rest = (nblk - 1) * A_BLOCK
        blocks = lambda t: t.reshape(d * (nblk - 1), A_BLOCK, 128)
        for p in range(A_GW // 128):
            cs = slice(p * 128, (p + 1) * 128)
            o, lb = head_pairs(
                q_ref[0, :, 0:A_BLOCK, cs],
                jnp.concatenate([kp_ref[0, :, :, cs], k_ref[0, :, 0:A_BLOCK, cs]], axis=1),
                jnp.concatenate([vp_ref[0, :, :, cs], v_ref[0, :, 0:A_BLOCK, cs]], axis=1), bias_block0)
            o_sc[p, :, 0:A_BLOCK, :] = o
            l_sc[p, :, 0:A_BLOCK, :] = lb
            if nblk > 1:
                o, lb = head_pairs(
                    blocks(q_ref[0, :, A_BLOCK:, cs]),
                    jnp.concatenate([blocks(k_ref[0, :, 0:rest, cs]), blocks(k_ref[0, :, A_BLOCK:, cs])], axis=1),
                    jnp.concatenate([blocks(v_ref[0, :, 0:rest, cs]), blocks(v_ref[0, :, A_BLOCK:, cs])], axis=1), bias)
                o_sc[p, :, A_BLOCK:, :] = o.reshape(d, rest, 128)
                l_sc[p, :, A_BLOCK:, :] = lb.reshape(d, rest, 128)

    for rho in range(DIL_MAX):
        for p in range(A_GW // 128):
            o_l = []
            for g, (_, d) in enumerate(A_GROUPS):
                rows = pl.ds(rho // d, A_BLOCK, stride=DIL_MAX // d) if d < DIL_MAX else slice(None)
                o_l.append((o_scs[g][p, rho % d, rows, :], l_scs[g][p, rho % d, rows, :]))
            y_ref[p, pl.ds(rho, A_BLOCK, stride=DIL_MAX), :] = _group_weights_combine(o_l)


def _attn_prompt(qs, ks, vs, nbatch, seq):
    nt = seq // TQ
    npair = A_GW // 128
    in_specs, args, scratch = [], [], []
    for g, (_, d) in enumerate(A_GROUPS):
        nblk = TQ // d // A_BLOCK
        cur = pl.BlockSpec((1, d, TQ // d, A_GW), lambda b, j: (b, 0, j, 0))
        prev = pl.BlockSpec((1, d, A_BLOCK, A_GW), lambda b, j, nblk=nblk: (b, 0, jnp.maximum(j * nblk - 1, 0), 0))
        in_specs += [cur, cur, prev, cur, prev]
        args += [qs[g], ks[g], ks[g], vs[g], vs[g]]
    for _ in range(2):
        scratch += [pltpu.VMEM((npair, d, TQ // d, 128), F32) for _, d in A_GROUPS]
    return pl.pallas_call(
        _attn_fused_kernel,
        grid=(nbatch, nt),
        in_specs=in_specs,
        out_specs=pl.BlockSpec((npair, TQ, 128), lambda b, j: (0, b * nt + j, 0)),
        out_shape=jax.ShapeDtypeStruct((npair, nbatch * seq, 128), F32),
        scratch_shapes=scratch,
        compiler_params=_params(2),
        name="attn_prompt",
    )(*args)


NEW_PAD = 128


def _cached_bias(T, n_cached, window, dil):
    t = np.arange(A_HPG * T)[:, None] % T
    jd_c = n_cached + t - np.arange(n_cached)[None, :]
    jd_n = t - np.arange(NEW_PAD)[None, :]
    ok_c = (jd_c % dil == 0) & (jd_c <= window)
    ok_n = (jd_n >= 0) & (jd_n % dil == 0) & (jd_n <= window)
    to_bias = lambda ok: jnp.asarray(np.where(ok, 0.0, -np.inf), F32)
    return to_bias(ok_c), to_bias(ok_n)


def _attn_cached_body(q_refs, kvn_ref, c_refs, bc_refs, bn_ref, y_ref, T):
    rows = A_HPG * T
    row_head = lax.broadcasted_iota(jnp.int32, (rows, A_GW), 0) // T
    lane_head = lax.broadcasted_iota(jnp.int32, (rows, A_GW), 1) // A_HD
    out_head = lax.broadcasted_iota(jnp.int32, (T, A_GW), 1) // A_HD
    pad = jnp.zeros((NEW_PAD - T, A_GW), F32)
    batches = range(kvn_ref.shape[0])
    units = [(j, g) for j in batches for g in range(len(A_GROUPS))]
    uid = range(len(units))
    new_rows = lambda j, c0: jnp.concatenate([kvn_ref[j, :, c0:c0 + A_GW], pad], axis=0).astype(BF16)
    qbd = [jnp.where(row_head == lane_head, jnp.concatenate([q_refs[g][j]] * A_HPG, axis=0), 0.0).astype(BF16)
           for j, g in units]
    kn = [new_rows(j, g * A_GW) for j, g in units]
    vn = [new_rows(j, A_WIDTH + g * A_GW) for j, g in units]
    yield
    k_t = [c_refs[g][j, 0:A_GW, :].astype(BF16) for j, g in units]
    yield
    s_c = [_dot(qbd[i], k_t[i]) + bc_refs[g][...] for i, (j, g) in enumerate(units)]
    s_n = [lax.dot_general(qbd[i], kn[i], NT_DIMS, preferred_element_type=F32) + bn_ref[g]
           for i, (j, g) in enumerate(units)]
    yield
    mx = [jnp.maximum(jnp.max(s_c[i], axis=-1, keepdims=True), jnp.max(s_n[i], axis=-1, keepdims=True)) for i in uid]
    yield
    e_c = [jnp.exp(s_c[i] - mx[i]) for i in uid]
    e_n = [jnp.exp(s_n[i] - mx[i]) for i in uid]
    yield
    den = [jnp.sum(e_c[i], axis=-1, keepdims=True) + jnp.sum(e_n[i], axis=-1, keepdims=True) for i in uid]
    v_t = [c_refs[g][j, A_GW:2 * A_GW, :].astype(BF16) for j, g in units]
    yield
    res = [(lax.dot_general(e_c[i].astype(BF16), v_t[i], NT_DIMS, preferred_element_type=F32)
            + _dot(e_n[i].astype(BF16), vn[i])) / den[i] for i in uid]
    lse = [mx[i] + jnp.log(den[i]) for i in uid]
    yield
    o_l = []
    for i in uid:
        o = jnp.zeros((T, A_GW), F32)
        lb = jnp.zeros((T, A_GW), F32)
        for h in range(A_HPG):
            o = jnp.where(out_head == h, res[i][h * T:(h + 1) * T, :], o)
            lb = jnp.where(out_head == h, lse[i][h * T:(h + 1) * T, :], lb)
        o_l.append((o, lb))
    ng = len(A_GROUPS)
    for j in batches:
        y_ref[j] = _group_weights_combine(o_l[j * ng:(j + 1) * ng])


def _attn_cached_rider(qs, kvn, caches, n_steps):
    nb, T, _ = kvn.shape
    nb_step = nb // n_steps
    per_b = lambda t: pl.BlockSpec((nb_step,) + t.shape[1:], lambda b: (b, 0, 0))
    biases = [_cached_bias(T, c.shape[2], w, d) for c, (w, d) in zip(caches, A_GROUPS)]
    bias_c = [bc for bc, _ in biases]
    bias_n = jnp.stack([bn for _, bn in biases])
    ng = len(A_GROUPS)

    def body(in_refs, out_refs):
        yield from _attn_cached_body(in_refs[0:ng], in_refs[ng], in_refs[ng + 1:2 * ng + 1],
                                     in_refs[2 * ng + 1:3 * ng + 1], in_refs[3 * ng + 1], out_refs[0], T)

    out = jax.ShapeDtypeStruct((nb, T, A_GW), F32)
    return _Rider("attn_cached", body, [*qs, kvn, *caches, *bias_c, bias_n],
                  [per_b(q) for q in qs] + [per_b(kvn)] + [per_b(c) for c in caches]
                  + [_resident(bc.shape) for bc in bias_c] + [_resident(bias_n.shape)],
                  [per_b(out)], [out], per_main=1)


POST_CHUNK = 256


def _post_body(get_x, hm, ya, gmix_ref, gmh_ref, wz_ref, wum_ref, wua_ref, wo_ref, store_out):
    cols = lambda c: slice(c * POST_CHUNK, (c + 1) * POST_CHUNK)
    h = _rms(get_x(slice(None)), gmix_ref[...]).astype(BF16)
    yield
    og = []
    for c in range(M_WIDTH // POST_CHUNK):
        og.append(_dot(h, wz_ref[:, cols(c)]))
        yield
    hm_act = (hm * gmh_ref[...] * jax.nn.sigmoid(jnp.concatenate(og, axis=1))).astype(BF16)
    ya_b = ya.astype(BF16)
    yield
    merged = []
    for c in range(D_MODEL // POST_CHUNK):
        gm = _dot(h, wz_ref[:, M_WIDTH + c * POST_CHUNK:M_WIDTH + (c + 1) * POST_CHUNK])
        yield
        ga = _dot(h, wz_ref[:, M_WIDTH + D_MODEL + c * POST_CHUNK:M_WIDTH + D_MODEL + (c + 1) * POST_CHUNK])
        yield
        y_m = _dot(hm_act, wum_ref[:, cols(c)])
        y_a = _dot(ya_b, wua_ref[:, cols(c)])
        yield
        merged.append((jax.nn.sigmoid(gm) * y_m + jax.nn.sigmoid(ga) * y_a).astype(BF16))
    merged = jnp.concatenate(merged, axis=1)
    for c in range(D_MODEL // POST_CHUNK):
        store_out(cols(c), get_x(cols(c)) + _dot(merged, wo_ref[:, cols(c)]))
        yield


def _post_kernel(x_ref, hm_ref, ya_ref, gmix_ref, gmh_ref, wz_ref, wum_ref, wua_ref, wo_ref, out_ref):
    ya = jnp.concatenate([ya_ref[p] for p in range(ya_ref.shape[0])], axis=1)

    def store_out(cols, value):
        out_ref[:, cols] = value

    for _ in _post_body(lambda cols: x_ref[:, cols], hm_ref[...], ya, gmix_ref, gmh_ref, wz_ref, wum_ref, wua_ref, wo_ref, store_out):
        pass


def _mixer_post_kernel(q_ref, k_ref, v_ref, gt_ref, bias_ref, c0_ref, n0_ref, m0_ref,
                       x_ref, ya_ref, gmix_ref, gmh_ref, wz_ref, wum_ref, wua_ref, wo_ref,
                       out_ref, c_ref, n_ref, m_ref, hm_ref, *, L, nchunk, nsteps):
    s = pl.program_id(0)
    cur = s % 2
    live = s < nsteps
    nb_step = q_ref.shape[0]
    rows = nb_step * L

    @pl.when(s == 0)
    def _():
        hm_ref[1] = jnp.zeros(hm_ref.shape[1:], F32)

    @pl.when(jnp.logical_and(live, s % nchunk == 0))
    def _():
        c_ref[...] = c0_ref[...]
        n_ref[...] = n0_ref[...]
        m_ref[...] = m0_ref[...]

    def store_h(nb, cols, value):
        hm_ref[cur, nb, :, cols] = value

    def store_out(cols, value):
        out_ref[:, :, cols] = value.reshape(nb_step, L, value.shape[1])

    get_x = lambda cols: x_ref[:, :, cols].reshape(rows, -1)
    hm = hm_ref[1 - cur].reshape(rows, M_WIDTH)
    ya = jnp.concatenate([ya_ref[p].reshape(rows, 128) for p in range(ya_ref.shape[0])], axis=1)
    mixer = _mlstm_body(q_ref, k_ref, v_ref, gt_ref, bias_ref, c_ref, n_ref, m_ref,
                        store_h, lambda write: pl.when(live)(write), L)
    post = _post_body(get_x, hm, ya, gmix_ref, gmh_ref, wz_ref, wum_ref, wua_ref, wo_ref, store_out)
    mixer_done = post_done = False
    while not (mixer_done and post_done):
        if not post_done:
            post_done = next(post, "done") == "done"
        if not mixer_done:
            mixer_done = next(mixer, "done") == "done"


def _mixer_post(q, k, v, gates, bias, c0, n0, m0, x, y_a, g_mix, g_mhead, w_z, w_up_m, w_up_a, w_out, L, nb_step):
    nb, seq, _ = q.shape
    nchunk = seq // L
    nsteps = (nb // nb_step) * nchunk
    cur = lambda s: (jnp.minimum(s, nsteps - 1) // nchunk, jnp.minimum(s, nsteps - 1) % nchunk)
    prv = lambda s: (jnp.maximum(s - 1, 0) // nchunk, jnp.maximum(s - 1, 0) % nchunk)
    tok = pl.BlockSpec((nb_step, L, M_WIDTH), lambda s: (*cur(s), 0))
    st_c = pl.BlockSpec((nb_step, M_HEADS, M_DV, M_DK), lambda s: (cur(s)[0], 0, 0, 0))
    st_n = pl.BlockSpec((nb_step, M_HEADS, M_DK), lambda s: (cur(s)[0], 0, 0))
    st_m = pl.BlockSpec((nb_step, M_HEADS, 1), lambda s: (cur(s)[0], 0, 0))
    x_spec = pl.BlockSpec((nb_step, L, D_MODEL), lambda s: (*prv(s), 0))
    return pl.pallas_call(
        functools.partial(_mixer_post_kernel, L=L, nchunk=nchunk, nsteps=nsteps),
        grid=(nsteps + 1,),
        in_specs=[tok, tok, tok, pl.BlockSpec((nb_step, 8, L), lambda s: (cur(s)[0], 0, cur(s)[1])),
                  _resident((8, 1)),
                  pl.BlockSpec(st_c.block_shape, st_c.index_map, pipeline_mode=pl.Buffered(1)), st_n, st_m,
                  x_spec, pl.BlockSpec((y_a.shape[0], nb_step, L, 128), lambda s: (0, *prv(s), 0)),
                  _resident((1, D_MODEL)), _resident((1, M_WIDTH)), _resident(w_z.shape),
                  _resident(w_up_m.shape), _resident(w_up_a.shape), _resident(w_out.shape)],
        out_specs=(x_spec, st_c, st_n, st_m),
        out_shape=(jax.ShapeDtypeStruct(x.shape, F32),
                   jax.ShapeDtypeStruct(c0.shape, F32), jax.ShapeDtypeStruct(n0.shape, F32),
                   jax.ShapeDtypeStruct(m0.shape, F32)),
        scratch_shapes=[pltpu.VMEM((2, nb_step, L, M_WIDTH), F32)],
        compiler_params=_params(1),
        name="mixer_post",
    )(q, k, v, gates, bias, c0, n0, m0, x, y_a, g_mix, g_mhead, w_z, w_up_m, w_up_a, w_out)


def _post(x, hm, y_a, g_mix, g_mhead, w_z, w_up_m, w_up_a, w_out, tm=1024):
    n = x.shape[0]
    tm = min(tm, n)
    row = lambda w: pl.BlockSpec((tm, w), lambda i: (i, 0))
    return pl.pallas_call(
        _post_kernel,
        grid=(n // tm,),
        in_specs=[row(D_MODEL), row(M_WIDTH), pl.BlockSpec((y_a.shape[0], tm, 128), lambda i: (0, i, 0)),
                  _resident((1, D_MODEL)), _resident((1, M_WIDTH)), _resident(w_z.shape),
                  _resident(w_up_m.shape), _resident(w_up_a.shape), _resident(w_out.shape)],
        out_specs=row(D_MODEL),
        out_shape=jax.ShapeDtypeStruct((n, D_MODEL), F32),
        compiler_params=_params(1),
        name="post",
    )(x, hm, y_a, g_mix, g_mhead, w_z, w_up_m, w_up_a, w_out)


def _rope_tables(pos):
    half = A_HD // 2
    inv_freq = ROPE_THETA ** (-2.0 * jnp.arange(half, dtype=F32) / A_HD)
    ang = pos.astype(F32)[:, None] * inv_freq[None, :]
    cos = jnp.cos(ang)
    sin = jnp.sin(ang)
    return jnp.tile(cos, (1, 4)), jnp.tile(jnp.concatenate([-sin, sin], axis=1), (1, 2))


def _window_rows(kvf, nbatch, rows):
    res = []
    for g, r in enumerate(rows):
        kk = kvf[:, kvf.shape[1] - r:, g * A_GW:(g + 1) * A_GW].reshape(nbatch, r, A_HPG, A_HD)
        vv = kvf[:, kvf.shape[1] - r:, A_WIDTH + g * A_GW:A_WIDTH + (g + 1) * A_GW].reshape(nbatch, r, A_HPG, A_HD)
        res.append(jnp.stack([kk, vv], axis=2))
    return res


def kernel(x_prompt, x_sample, state_mlstm_C, state_mlstm_n, state_mlstm_m, cache_win128_kv, cache_win512_kv, cache_win2048_kv, g_ffn1, w1_gate, w1_up, w1_down, g_mix, w_in, b_igate, b_fgate, g_mhead, w_up_m, w_up_a, w_out, g_ffn2, w2_gate, w2_up, w2_down, g_final):
    B, S, _ = x_prompt.shape
    BS, T, _ = x_sample.shape
    depth = w_in.shape[0]
    caches_all = (cache_win128_kv, cache_win512_kv, cache_win2048_kv)

    cos_p, sin_p = _rope_tables(jnp.arange(S, dtype=jnp.int32))
    cos_s, sin_s = _rope_tables(PAST_LEN + jnp.arange(T, dtype=jnp.int32))
    cos_s = jnp.tile(cos_s, (BS, 1))
    sin_s = jnp.tile(sin_s, (BS, 1))

    xp = x_prompt.reshape(B * S, D_MODEL)
    xs = x_sample.reshape(BS * T, D_MODEL)
    L_P = 256
    L_S = 128
    NB_P = 4
    FFN1_TM, FFN2_TM = 1024, 512
    prompt_rows = tuple(min(w, S) for w, _ in A_GROUPS)
    tail = max(prompt_rows)
    dils = tuple(d for _, d in A_GROUPS)
    acc = [[] for _ in range(12)]
    for l in range(depth):
        bf = lambda w: w[l].astype(BF16)
        vec = lambda g: g[l].reshape(1, -1)
        wi = w_in[l]
        w_qkv = jnp.concatenate([wi[:, _C_QM:_C_IG], wi[:, _C_QA:_C_GM]], axis=1).astype(BF16)
        w_gate_t = wi[:, _C_IG:_C_OG].T.astype(BF16)
        w_z = jnp.concatenate([wi[:, _C_OG:_C_QA], wi[:, _C_GM:_C_END]], axis=1).astype(BF16)
        ffn1 = (vec(g_ffn1), bf(w1_gate), bf(w1_up), bf(w1_down))
        ffn2 = (vec(g_ffn2), bf(w2_gate), bf(w2_up), bf(w2_down))
        post_w = (vec(g_mix), vec(g_mhead), w_z, bf(w_up_m), bf(w_up_a), bf(w_out))
        gate_bias = jnp.concatenate([b_igate[l], b_fgate[l]]).reshape(8, 1)
        last = l == depth - 1
        gfin = g_final.reshape(1, -1) if last else None

        xs = _ffn(xs, *ffn1)
        qm, km, vm, *qkv_s, kvf_s, gt = _inproj(xs, vec(g_mix), w_qkv, w_gate_t, cos_s, sin_s, 1, BS * T, BS * T,
                                                (1,) * len(A_GROUPS))
        kvf_s = kvf_s.reshape(BS, T, 2 * A_WIDTH)
        pad_tok = lambda t: jnp.pad(t.reshape(BS, T, M_WIDTH), ((0, 0), (0, L_S - T), (0, 0)))
        gt = gt.reshape(8, BS, T).transpose(1, 0, 2)
        gt = jnp.concatenate([jnp.pad(gt[:, 0:4], ((0, 0), (0, 0), (0, L_S - T)), constant_values=-1e30),
                              jnp.pad(gt[:, 4:8], ((0, 0), (0, 0), (0, L_S - T)), constant_values=1e30)], axis=1)
        mlstm_s = _mlstm_rider(pad_tok(qm), pad_tok(km), pad_tok(vm), gt, gate_bias, state_mlstm_C[l],
                               state_mlstm_n[l], state_mlstm_m[l].reshape(BS, M_HEADS, 1), B * S // FFN1_TM)
        caches = [jnp.transpose(c[l], (0, 2, 3, 4, 1)).reshape(BS, 2 * A_GW, c.shape[2]) for c in caches_all]
        attn_s = _attn_cached_rider([q.astype(F32).reshape(BS, T, A_GW) for q in qkv_s[0:3]], kvf_s, caches,
                                    B * S // FFN2_TM)

        xp, (hm_s, c_s, n_s, m_s) = _ffn(xp, *ffn1, tm=FFN1_TM, rider=mlstm_s)
        qm, km, vm, *qkv_a, kvf, gt = _inproj(xp, vec(g_mix), w_qkv, w_gate_t, cos_p, sin_p, B, S, tail, dils)
        y_a = _attn_prompt(qkv_a[0:3], qkv_a[3:6], qkv_a[6:9], B, S)
        xp, c_p, n_p, m_p = _mixer_post(
            qm.reshape(B, S, M_WIDTH), km.reshape(B, S, M_WIDTH), vm.reshape(B, S, M_WIDTH), gt, gate_bias,
            jnp.zeros((B, M_HEADS, M_DV, M_DK), F32), jnp.zeros((B, M_HEADS, M_DK), F32),
            jnp.zeros((B, M_HEADS, 1), F32), xp.reshape(B, S, D_MODEL), y_a.reshape(y_a.shape[0], B, S, 128),
            *post_w, L_P, NB_P)
        xp = xp.reshape(B * S, D_MODEL)
        xp, (ya_s,) = _ffn(xp, *ffn2, g_final=gfin, tm=FFN2_TM, rider=attn_s)
        kv_p = _window_rows(kvf, B, prompt_rows)
        for i, t in enumerate((c_p, n_p, m_p.reshape(B, M_HEADS), *kv_p)):
            acc[i].append(t)

        hm_s = hm_s[:, :T].reshape(BS * T, M_WIDTH)
        xs = _post(xs, hm_s, ya_s.reshape(BS * T, A_GW // 128, 128).transpose(1, 0, 2), *post_w)
        xs = _ffn(xs, *ffn2, g_final=gfin)
        kv_s = _window_rows(kvf_s, BS, (T,) * len(A_GROUPS))
        for i, t in enumerate((c_s, n_s, m_s.reshape(BS, M_HEADS), *kv_s)):
            acc[6 + i].append(t)

    stacked = [jnp.stack(a) for a in acc]
    return (xp.reshape(B, S, D_MODEL), xs.reshape(BS, T, D_MODEL), *stacked)
```

```python
import functools

import jax
import jax.numpy as jnp
import numpy as np
from jax import lax
from jax.experimental import pallas as pl
from jax.experimental.pallas import tpu as pltpu

F32 = jnp.float32
BF16 = jnp.bfloat16

D_MODEL = 1024
D_FF = 2816
PAST_LEN = 8192
M_HEADS = 4
M_DK = 128
M_DV = 128
M_WIDTH = M_HEADS * M_DV
A_GROUPS = ((128, 1), (512, 4), (2048, 16))
A_HPG = 4
A_HD = 64
A_GW = A_HPG * A_HD
A_WIDTH = len(A_GROUPS) * A_GW
A_BLOCK = 128
ROPE_THETA = 10000.0
EPS = 1e-6
NEG_INF = float("-inf")

_C_QM, _C_KM, _C_VM, _C_IG, _C_FG, _C_OG = 0, 512, 1024, 1536, 1540, 1544
_C_QA, _C_KA, _C_VA, _C_GM, _C_GA, _C_END = 2056, 2824, 3592, 4360, 5384, 6408

VMEM_LIMIT = 60 * 1024 * 1024
NT_DIMS = (((1,), (1,)), ((), ()))
TN_DIMS = (((0,), (0,)), ((), ()))


def _params(n_axes):
    return pltpu.CompilerParams(dimension_semantics=("arbitrary",) * n_axes,
                                vmem_limit_bytes=VMEM_LIMIT)


def _resident(shape):
    return pl.BlockSpec(shape, lambda *_: (0,) * len(shape), pipeline_mode=pl.Buffered(1))


def _rms(x, g):
    return x * lax.rsqrt(jnp.mean(x * x, axis=-1, keepdims=True) + EPS) * g


def _dot(a, b):
    return jnp.dot(a, b, preferred_element_type=F32)


FF_CHUNK = 256


def _interleave(main, rider, rider_per_main):
    main_done = rider_done = False
    while not (main_done and rider_done):
        if not main_done:
            main_done = next(main, "done") == "done"
        for _ in range(rider_per_main):
            if not rider_done:
                rider_done = next(rider, "done") == "done"


def _ffn_body(x_ref, g_ref, wg_ref, wu_ref, wd_ref, gf_ref, o_ref, act_ref):
    x = x_ref[...]
    h = _rms(x, g_ref[...]).astype(BF16)
    yield
    for c in range(D_FF // FF_CHUNK):
        sl = slice(c * FF_CHUNK, (c + 1) * FF_CHUNK)
        gate = _dot(h, wg_ref[:, sl])
        up = _dot(h, wu_ref[:, sl])
        act_ref[:, sl] = (gate * jax.nn.sigmoid(gate) * up).astype(BF16)
        yield
    out = x + 0.5 * _dot(act_ref[...], wd_ref[...])
    if gf_ref is not None:
        out = _rms(out, gf_ref[...])
    o_ref[...] = out


def _ffn_kernel(*refs, final_norm, rider):
    n_in = 6 if final_norm else 5
    x_ref, g_ref, wg_ref, wu_ref, wd_ref = refs[:5]
    gf_ref = refs[5] if final_norm else None
    n_rin, n_rout = (rider.n_in, rider.n_out) if rider else (0, 0)
    rider_in = refs[n_in:n_in + n_rin]
    o_ref = refs[n_in + n_rin]
    rider_out = refs[n_in + n_rin + 1:n_in + n_rin + 1 + n_rout]
    act_ref = refs[-1]
    main = _ffn_body(x_ref, g_ref, wg_ref, wu_ref, wd_ref, gf_ref, o_ref, act_ref)
    if rider:
        _interleave(main, rider.body(rider_in, rider_out), rider.per_main)
    else:
        for _ in main:
            pass


class _Rider:
    def __init__(self, name, body, args, in_specs, out_specs, out_shapes, per_main):
        self.name, self.body, self.args, self.per_main = name, body, args, per_main
        self.in_specs, self.out_specs, self.out_shapes = in_specs, out_specs, out_shapes
        self.n_in, self.n_out = len(in_specs), len(out_specs)


def _ffn(x, g, wg, wu, wd, g_final=None, tm=1024, rider=None):
    n = x.shape[0]
    tm = min(tm, n)
    final_norm = g_final is not None
    row = pl.BlockSpec((tm, D_MODEL), lambda i: (i, 0))
    in_specs = [row, _resident((1, D_MODEL)), _resident((D_MODEL, D_FF)),
                _resident((D_MODEL, D_FF)), _resident((D_FF, D_MODEL))]
    args = [x, g, wg, wu, wd]
    if final_norm:
        in_specs.append(_resident((1, D_MODEL)))
        args.append(g_final)
    out_specs, out_shapes = [row], [jax.ShapeDtypeStruct((n, D_MODEL), F32)]
    name = "ffn_final" if final_norm else "ffn"
    if rider:
        in_specs += rider.in_specs
        args += rider.args
        out_specs += rider.out_specs
        out_shapes += rider.out_shapes
        name += "_" + rider.name
    res = pl.pallas_call(
        functools.partial(_ffn_kernel, final_norm=final_norm, rider=rider),
        grid=(n // tm,),
        in_specs=in_specs,
        out_specs=tuple(out_specs),
        out_shape=tuple(out_shapes),
        scratch_shapes=[pltpu.VMEM((tm, D_FF), BF16)],
        compiler_params=_params(1),
        name=name,
    )(*args)
    return (res[0], res[1:]) if rider else res[0]


def _rope(t, cos, sin):
    lane = lax.broadcasted_iota(jnp.int32, cos.shape, 1)
    first_half = (lane % A_HD) < (A_HD // 2)
    outs = []
    for c in range(t.shape[1] // 128):
        tc = t[:, c * 128:(c + 1) * 128]
        partner = jnp.where(first_half, pltpu.roll(tc, 128 - A_HD // 2, 1), pltpu.roll(tc, A_HD // 2, 1))
        outs.append(tc * cos + partner * sin)
    return jnp.concatenate(outs, axis=1)


def _inproj_kernel(x_ref, g_ref, w_ref, wgt_ref, cos_ref, sin_ref,
                   qm_ref, km_ref, vm_ref, q1_ref, q2_ref, q3_ref, k1_ref, k2_ref, k3_ref,
                   v1_ref, v2_ref, v3_ref, kvf_ref, gt_ref, z_ref, *, dils):
    tm = x_ref.shape[0]

    def emit(z, refs):
        for c in range(A_WIDTH // 128):
            z_ref[c] = z[:, c * 128:(c + 1) * 128]
        per_group = A_GW // 128
        for g, (ref, d) in enumerate(zip(refs, dils)):
            for r in range(d):
                for p in range(per_group):
                    ref[0, r, :, p * 128:(p + 1) * 128] = z_ref[
                        g * per_group + p, pl.ds(r, tm // d, stride=d), :].astype(BF16)

    h = _rms(x_ref[...], g_ref[...]).astype(BF16)
    qm_ref[...] = _dot(h, w_ref[:, 0:512]).astype(BF16)
    km_ref[...] = (_dot(h, w_ref[:, 512:1024]) * (M_DK ** -0.5)).astype(BF16)
    vm_ref[...] = _dot(h, w_ref[:, 1024:1536]).astype(BF16)
    cos = cos_ref[...]
    sin = sin_ref[...]
    emit(_rope(_dot(h, w_ref[:, 1536:2304]), cos, sin) * (A_HD ** -0.5), (q1_ref, q2_ref, q3_ref))
    ka = _rope(_dot(h, w_ref[:, 2304:3072]), cos, sin)
    emit(ka, (k1_ref, k2_ref, k3_ref))
    va = _dot(h, w_ref[:, 3072:3840])
    emit(va, (v1_ref, v2_ref, v3_ref))
    kvf_ref[0, :, 0:A_WIDTH] = ka
    kvf_ref[0, :, A_WIDTH:2 * A_WIDTH] = va
    gt_ref[0] = lax.dot_general(wgt_ref[...], h, NT_DIMS, preferred_element_type=F32)


def _inproj(x, g, w_qkv, w_gate_t, cos, sin, nbatch, seq, tail, dils, tm=1024):
    n = x.shape[0]
    tm = min(tm, tail)
    tpb = seq // tm
    first_tail = tpb - tail // tm
    row = lambda w: pl.BlockSpec((tm, w), lambda i: (i, 0))
    table = pl.BlockSpec((tm, 128), lambda i: (i % tpb, 0))
    grouped_shape = [jax.ShapeDtypeStruct((nbatch, d, seq // d, A_GW), BF16) for d in dils]
    grouped_spec = [pl.BlockSpec((1, d, tm // d, A_GW), lambda i: (i // tpb, 0, i % tpb, 0)) for d in dils]
    out_shapes = (
        jax.ShapeDtypeStruct((n, 512), BF16), jax.ShapeDtypeStruct((n, 512), BF16),
        jax.ShapeDtypeStruct((n, 512), BF16),
        *grouped_shape, *grouped_shape, *grouped_shape,
        jax.ShapeDtypeStruct((nbatch, tail, 2 * A_WIDTH), F32),
        jax.ShapeDtypeStruct((nbatch, 8, seq), F32),
    )
    out_specs = (
        row(512), row(512), row(512), *grouped_spec, *grouped_spec, *grouped_spec,
        pl.BlockSpec((1, tm, 2 * A_WIDTH), lambda i: (i // tpb, jnp.maximum(i % tpb - first_tail, 0), 0)),
        pl.BlockSpec((1, 8, tm), lambda i: (i // tpb, 0, i % tpb)),
    )
    return pl.pallas_call(
        functools.partial(_inproj_kernel, dils=dils),
        grid=(n // tm,),
        in_specs=[row(D_MODEL), _resident((1, D_MODEL)), _resident(w_qkv.shape),
                  _resident(w_gate_t.shape), table, table],
        out_specs=out_specs,
        out_shape=out_shapes,
        scratch_shapes=[pltpu.VMEM((A_WIDTH // 128, tm, 128), F32)],
        compiler_params=_params(1),
        name="inproj",
    )(x, g, w_qkv, w_gate_t, cos, sin)


def _scan_lanes(xs, lane, op, identity):
    shift = 1
    while shift < xs[0].shape[1]:
        xs = [op(x, jnp.where(lane >= shift, pltpu.roll(x, shift, 1), identity)) for x in xs]
        shift *= 2
    return xs


def _mlstm_body(q_ref, k_ref, v_ref, gt_ref, bias_ref, c_ref, n_ref, m_ref, store_h, store_state, L):
    seqs = range(q_ref.shape[0])
    units = [(nb, h) for nb in seqs for h in range(M_HEADS)]
    uid = range(len(units))
    hs = lambda h: slice(h * M_DK, (h + 1) * M_DK)
    r_idx = lax.broadcasted_iota(jnp.int32, (L, L), 0)
    c_idx = lax.broadcasted_iota(jnp.int32, (L, L), 1)
    causal = c_idx <= r_idx
    lane = lax.broadcasted_iota(jnp.int32, (M_HEADS, L), 1)
    bias = bias_ref[...]

    gates = [gt_ref[nb] + bias for nb in seqs]
    ig = [g[0:4, :] for g in gates]
    lf = [jnp.minimum(g[4:8, :], 0.0) - jnp.log1p(jnp.exp(-jnp.abs(g[4:8, :]))) for g in gates]
    yield
    b = _scan_lanes(lf, lane, jnp.add, 0.0)
    yield
    a = [ig[nb] - b[nb] for nb in seqs]
    m_prev = [m_ref[nb] for nb in seqs]
    a_max = _scan_lanes(a, lane, jnp.maximum, NEG_INF)
    yield
    big_m = [jnp.maximum(m_prev[nb], a_max[nb]) for nb in seqs]
    m_t = [b[nb] + big_m[nb] for nb in seqs]
    w_inter = [jnp.exp(m_prev[nb] - big_m[nb]) for nb in seqs]
    e_negm = [jnp.exp(-m_t[nb]) for nb in seqs]
    b_last = [b[nb][:, L - 1:L] for nb in seqs]
    m_new = [m_t[nb][:, L - 1:L] for nb in seqs]
    decay = [jnp.exp(b_last[nb] + m_prev[nb] - m_new[nb]) for nb in seqs]
    w_s = [jnp.exp(a[nb] + (b_last[nb] - m_new[nb])) for nb in seqs]
    yield
    pad = jnp.zeros((128 - 4 * M_HEADS, L), F32)
    cols = [jnp.transpose(jnp.concatenate([big_m[nb], w_inter[nb], e_negm[nb], w_s[nb], pad], axis=0)) for nb in seqs]
    col = lambda j: [cols[nb][:, 4 * j + h:4 * j + h + 1] for nb, h in units]
    big_m_col, w_inter_col, e_negm_col, w_s_col = col(0), col(1), col(2), col(3)
    yield

    q = [q_ref[nb, :, hs(h)] for nb, h in units]
    k = [k_ref[nb, :, hs(h)] for nb, h in units]
    v = [v_ref[nb, :, hs(h)] for nb, h in units]
    c_old = [c_ref[nb, h] for nb, h in units]
    n_old = [n_ref[nb, h:h + 1, :] for nb, h in units]
    s = [lax.dot_general(q[i], k[i], NT_DIMS, preferred_element_type=F32) for i in uid]
    yield
    inter = [lax.dot_general(q[i], c_old[i].astype(BF16), NT_DIMS, preferred_element_type=F32) for i in uid]
    yield
    vw = [(v[i].astype(F32) * w_s_col[i]).astype(BF16) for i in uid]
    yield
    upd = [lax.dot_general(vw[i], k[i], TN_DIMS, preferred_element_type=F32) for i in uid]
    yield
    w_rows = [jnp.broadcast_to(w_s[nb][h:h + 1, :], (8, L)).astype(BF16) for nb, h in units]
    n_upd = [_dot(w_rows[i], k[i])[0:1, :] for i in uid]
    yield
    c_new = [decay[nb][h:h + 1, :] * c_old[i] + upd[i] for i, (nb, h) in enumerate(units)]
    n_new = [decay[nb][h:h + 1, :] * n_old[i] + n_upd[i] for i, (nb, h) in enumerate(units)]
    yield
    p = [s[i] * jnp.exp(jnp.where(causal, a[nb][h:h + 1, :] - big_m_col[i], NEG_INF))
         for i, (nb, h) in enumerate(units)]
    yield
    pv = [_dot(p[i].astype(BF16), v[i]) for i in uid]
    yield
    num = [w_inter_col[i] * inter[i] + pv[i] for i in uid]
    yield
    qn = [jnp.sum(q[i].astype(F32) * n_old[i], axis=-1, keepdims=True) for i in uid]
    yield
    den = [w_inter_col[i] * qn[i] + jnp.sum(p[i], axis=-1, keepdims=True) for i in uid]
    yield
    hh = [num[i] / jnp.maximum(jnp.abs(den[i]), e_negm_col[i]) for i in uid]
    yield
    hn = [hh[i] * lax.rsqrt(jnp.mean(hh[i] * hh[i], axis=-1, keepdims=True) + EPS) for i in uid]
    yield
    for i, (nb, h) in enumerate(units):
        store_h(nb, hs(h), hn[i])

    def write_state():
        for i, (nb, h) in enumerate(units):
            c_ref[nb, h] = c_new[i]
            n_ref[nb, h:h + 1, :] = n_new[i]
        for nb in seqs:
            m_ref[nb] = m_new[nb]
    store_state(write_state)


def _mlstm_rider(q, k, v, gates, bias, c0, n0, m0, n_steps):
    nb, L, _ = q.shape
    nb_step = nb // n_steps
    blk = lambda t: pl.BlockSpec((nb_step,) + t.shape[1:], lambda i: (i,) + (0,) * (t.ndim - 1))

    def body(in_refs, out_refs):
        q_ref, k_ref, v_ref, gt_ref, bias_ref, c0_ref, n0_ref, m0_ref = in_refs
        h_ref, c_ref, n_ref, m_ref = out_refs
        c_ref[...] = c0_ref[...]
        n_ref[...] = n0_ref[...]
        m_ref[...] = m0_ref[...]

        def store_h(nb, cols, value):
            h_ref[nb, :, cols] = value

        yield from _mlstm_body(q_ref, k_ref, v_ref, gt_ref, bias_ref, c_ref, n_ref, m_ref,
                               store_h, lambda write: write(), L)

    outs = (jax.ShapeDtypeStruct((nb, L, M_WIDTH), F32), jax.ShapeDtypeStruct(c0.shape, F32),
            jax.ShapeDtypeStruct(n0.shape, F32), jax.ShapeDtypeStruct(m0.shape, F32))
    return _Rider("mlstm", body, [q, k, v, gates, bias, c0, n0, m0],
                  [blk(q), blk(k), blk(v), blk(gates), _resident(bias.shape), blk(c0), blk(n0), blk(m0)],
                  [blk(o) for o in outs], list(outs), per_main=2)


DIL_MAX = max(d for _, d in A_GROUPS)
TQ = A_BLOCK * DIL_MAX


def _group_weights_combine(o_l):
    mx = functools.reduce(jnp.maximum, [l for _, l in o_l])
    es = [jnp.exp(l - mx) for _, l in o_l]
    num = functools.reduce(jnp.add, [e * o for e, (o, _) in zip(es, o_l)])
    return num / functools.reduce(jnp.add, es)


def _attn_fused_kernel(*refs):
    ng = len(A_GROUPS)
    ins = [refs[5 * g:5 * g + 5] for g in range(ng)]
    y_ref = refs[5 * ng]
    o_scs = refs[5 * ng + 1:5 * ng + 1 + ng]
    l_scs = refs[5 * ng + 1 + ng:]
    not_first = pl.program_id(1) > 0
    q_t = lax.broadcasted_iota(jnp.int32, (2 * A_BLOCK, 2 * A_BLOCK), 0) % A_BLOCK
    key = lax.broadcasted_iota(jnp.int32, (2 * A_BLOCK, 2 * A_BLOCK), 1)
    band = jnp.logical_and(key >= q_t, key <= q_t + A_BLOCK)
    bias = jnp.where(band, 0.0, NEG_INF)
    bias_block0 = jnp.where(jnp.logical_and(band, jnp.logical_or(key >= A_BLOCK, not_first)), 0.0, NEG_INF)
    low = lax.broadcasted_iota(jnp.int32, (A_BLOCK, 2 * A_HD), 1) < A_HD
    batch_nt = (((2,), (2,)), ((0,), (0,)))
    batch_nn = (((2,), (1,)), ((0,), (0,)))

    def head_pairs(qp, ks, vs, mask_bias):
        zero = jnp.zeros_like(qp)
        q2 = jnp.concatenate([jnp.where(low, qp, zero), jnp.where(low, zero, qp)], axis=1)
        s = lax.dot_general(q2, ks, batch_nt, preferred_element_type=F32) + mask_bias
        mx = jnp.max(s, axis=-1, keepdims=True)
        e = jnp.exp(s - mx)
        den = jnp.sum(e, axis=-1, keepdims=True)
        o2 = lax.dot_general(e.astype(BF16), vs, batch_nn, preferred_element_type=F32) / den
        lse = mx + jnp.log(den)
        o = jnp.where(low, o2[:, 0:A_BLOCK], o2[:, A_BLOCK:])
        lb = jnp.where(low, jnp.broadcast_to(lse[:, 0:A_BLOCK], o.shape), jnp.broadcast_to(lse[:, A_BLOCK:], o.shape))
        return o, lb

    for g, (_, d) in enumerate(A_GROUPS):
        q_ref, k_ref, kp_ref, v_ref, vp_ref = ins[g]
        o_sc, l_sc = o_scs[g], l_scs[g]
        nblk = TQ // d // A_BLOCK
        rest = (nblk - 1) * A_BLOCK
        blocks = lambda t: t.reshape(d * (nblk - 1), A_BLOCK, 128)
        for p in range(A_GW // 128):
            cs = slice(p * 128, (p + 1) * 128)
            o, lb = head_pairs(
                q_ref[0, :, 0:A_BLOCK, cs],
                jnp.concatenate([kp_ref[0, :, :, cs], k_ref[0, :, 0:A_BLOCK, cs]], axis=1),
                jnp.concatenate([vp_ref[0, :, :, cs], v_ref[0, :, 0:A_BLOCK, cs]], axis=1), bias_block0)
            o_sc[p, :, 0:A_BLOCK, :] = o
            l_sc[p, :, 0:A_BLOCK, :] = lb
            if nblk > 1:
                o, lb = head_pairs(
                    blocks(q_ref[0, :, A_BLOCK:, cs]),
                    jnp.concatenate([blocks(k_ref[0, :, 0:rest, cs]), blocks(k_ref[0, :, A_BLOCK:, cs])], axis=1),
                    jnp.concatenate([blocks(v_ref[0, :, 0:rest, cs]), blocks(v_ref[0, :, A_BLOCK:, cs])], axis=1), bias)
                o_sc[p, :, A_BLOCK:, :] = o.reshape(d, rest, 128)
                l_sc[p, :, A_BLOCK:, :] = lb.reshape(d, rest, 128)

    for rho in range(DIL_MAX):
        for p in range(A_GW // 128):
            o_l = []
            for g, (_, d) in enumerate(A_GROUPS):
                rows = pl.ds(rho // d, A_BLOCK, stride=DIL_MAX // d) if d < DIL_MAX else slice(None)
                o_l.append((o_scs[g][p, rho % d, rows, :], l_scs[g][p, rho % d, rows, :]))
            y_ref[p, pl.ds(rho, A_BLOCK, stride=DIL_MAX), :] = _group_weights_combine(o_l)


def _attn_prompt(qs, ks, vs, nbatch, seq):
    nt = seq // TQ
    npair = A_GW // 128
    in_specs, args, scratch = [], [], []
    for g, (_, d) in enumerate(A_GROUPS):
        nblk = TQ // d // A_BLOCK
        cur = pl.BlockSpec((1, d, TQ // d, A_GW), lambda b, j: (b, 0, j, 0))
        prev = pl.BlockSpec((1, d, A_BLOCK, A_GW), lambda b, j, nblk=nblk: (b, 0, jnp.maximum(j * nblk - 1, 0), 0))
        in_specs += [cur, cur, prev, cur, prev]
        args += [qs[g], ks[g], ks[g], vs[g], vs[g]]
    for _ in range(2):
        scratch += [pltpu.VMEM((npair, d, TQ // d, 128), F32) for _, d in A_GROUPS]
    return pl.pallas_call(
        _attn_fused_kernel,
        grid=(nbatch, nt),
        in_specs=in_specs,
        out_specs=pl.BlockSpec((npair, TQ, 128), lambda b, j: (0, b * nt + j, 0)),
        out_shape=jax.ShapeDtypeStruct((npair, nbatch * seq, 128), F32),
        scratch_shapes=scratch,
        compiler_params=_params(2),
        name="attn_prompt",
    )(*args)


NEW_PAD = 128


def _cached_bias(T, n_cached, window, dil):
    t = np.arange(A_HPG * T)[:, None] % T
    jd_c = n_cached + t - np.arange(n_cached)[None, :]
    jd_n = t - np.arange(NEW_PAD)[None, :]
    ok_c = (jd_c % dil == 0) & (jd_c <= window)
    ok_n = (jd_n >= 0) & (jd_n % dil == 0) & (jd_n <= window)
    to_bias = lambda ok: jnp.asarray(np.where(ok, 0.0, -np.inf), F32)
    return to_bias(ok_c), to_bias(ok_n)


def _attn_cached_body(q_refs, kvn_ref, c_refs, bc_refs, bn_ref, y_ref, T):
    rows = A_HPG * T
    row_head = lax.broadcasted_iota(jnp.int32, (rows, A_GW), 0) // T
    lane_head = lax.broadcasted_iota(jnp.int32, (rows, A_GW), 1) // A_HD
    out_head = lax.broadcasted_iota(jnp.int32, (T, A_GW), 1) // A_HD
    pad = jnp.zeros((NEW_PAD - T, A_GW), F32)
    batches = range(kvn_ref.shape[0])
    units = [(j, g) for j in batches for g in range(len(A_GROUPS))]
    uid = range(len(units))
    new_rows = lambda j, c0: jnp.concatenate([kvn_ref[j, :, c0:c0 + A_GW], pad], axis=0).astype(BF16)
    qbd = [jnp.where(row_head == lane_head, jnp.concatenate([q_refs[g][j]] * A_HPG, axis=0), 0.0).astype(BF16)
           for j, g in units]
    kn = [new_rows(j, g * A_GW) for j, g in units]
    vn = [new_rows(j, A_WIDTH + g * A_GW) for j, g in units]
    yield
    k_t = [c_refs[g][j, 0:A_GW, :].astype(BF16) for j, g in units]
    yield
    s_c = [_dot(qbd[i], k_t[i]) + bc_refs[g][...] for i, (j, g) in enumerate(units)]
    s_n = [lax.dot_general(qbd[i], kn[i], NT_DIMS, preferred_element_type=F32) + bn_ref[g]
           for i, (j, g) in enumerate(units)]
    yield
    mx = [jnp.maximum(jnp.max(s_c[i], axis=-1, keepdims=True), jnp.max(s_n[i], axis=-1, keepdims=True)) for i in uid]
    yield
    e_c = [jnp.exp(s_c[i] - mx[i]) for i in uid]
    e_n = [jnp.exp(s_n[i] - mx[i]) for i in uid]
    yield
    den = [jnp.sum(e_c[i], axis=-1, keepdims=True) + jnp.sum(e_n[i], axis=-1, keepdims=True) for i in uid]
    v_t = [c_refs[g][j, A_GW:2 * A_GW, :].astype(BF16) for j, g in units]
    yield
    res = [(lax.dot_general(e_c[i].astype(BF16), v_t[i], NT_DIMS, preferred_element_type=F32)
            + _dot(e_n[i].astype(BF16), vn[i])) / den[i] for i in uid]
    lse = [mx[i] + jnp.log(den[i]) for i in uid]
    yield
    o_l = []
    for i in uid:
        o = jnp.zeros((T, A_GW), F32)
        lb = jnp.zeros((T, A_GW), F32)
        for h in range(A_HPG):
            o = jnp.where(out_head == h, res[i][h * T:(h + 1) * T, :], o)
            lb = jnp.where(out_head == h, lse[i][h * T:(h + 1) * T, :], lb)
        o_l.append((o, lb))
    ng = len(A_GROUPS)
    for j in batches:
        y_ref[j] = _group_weights_combine(o_l[j * ng:(j + 1) * ng])


def _attn_cached_rider(qs, kvn, caches, n_steps):
    nb, T, _ = kvn.shape
    nb_step = nb // n_steps
    per_b = lambda t: pl.BlockSpec((nb_step,) + t.shape[1:], lambda b: (b, 0, 0))
    biases = [_cached_bias(T, c.shape[2], w, d) for c, (w, d) in zip(caches, A_GROUPS)]
    bias_c = [bc for bc, _ in biases]
    bias_n = jnp.stack([bn for _, bn in biases])
    ng = len(A_GROUPS)

    def body(in_refs, out_refs):
        yield from _attn_cached_body(in_refs[0:ng], in_refs[ng], in_refs[ng + 1:2 * ng + 1],
                                     in_refs[2 * ng + 1:3 * ng + 1], in_refs[3 * ng + 1], out_refs[0], T)

    out = jax.ShapeDtypeStruct((nb, T, A_GW), F32)
    return _Rider("attn_cached", body, [*qs, kvn, *caches, *bias_c, bias_n],
                  [per_b(q) for q in qs] + [per_b(kvn)] + [per_b(c) for c in caches]
                  + [_resident(bc.shape) for bc in bias_c] + [_resident(bias_n.shape)],
                  [per_b(out)], [out], per_main=1)


POST_CHUNK = 256


def _post_body(get_x, get_hm, get_ya, gmix_ref, gmh_ref, wz_ref, wum_ref, wua_ref, wo_ref, store_out):
    cols = lambda c: slice(c * POST_CHUNK, (c + 1) * POST_CHUNK)
    h = _rms(get_x(slice(None)), gmix_ref[...]).astype(BF16)
    yield
    og = []
    for c in range(M_WIDTH // POST_CHUNK):
        og.append(_dot(h, wz_ref[:, cols(c)]))
        yield
    hm_act = (get_hm() * gmh_ref[...] * jax.nn.sigmoid(jnp.concatenate(og, axis=1))).astype(BF16)
    ya_b = get_ya().astype(BF16)
    yield
    merged = []
    for c in range(D_MODEL // POST_CHUNK):
        gm = _dot(h, wz_ref[:, M_WIDTH + c * POST_CHUNK:M_WIDTH + (c + 1) * POST_CHUNK])
        yield
        ga = _dot(h, wz_ref[:, M_WIDTH + D_MODEL + c * POST_CHUNK:M_WIDTH + D_MODEL + (c + 1) * POST_CHUNK])
        yield
        y_m = _dot(hm_act, wum_ref[:, cols(c)])
        y_a = _dot(ya_b, wua_ref[:, cols(c)])
        yield
        merged.append((jax.nn.sigmoid(gm) * y_m + jax.nn.sigmoid(ga) * y_a).astype(BF16))
    merged = jnp.concatenate(merged, axis=1)
    for c in range(2):
        wide = slice(c * D_MODEL // 2, (c + 1) * D_MODEL // 2)
        store_out(wide, get_x(wide) + _dot(merged, wo_ref[:, wide]))
        yield


def _post_kernel(x_ref, hm_ref, ya_ref, gmix_ref, gmh_ref, wz_ref, wum_ref, wua_ref, wo_ref, out_ref):
    get_ya = lambda: jnp.concatenate([ya_ref[p] for p in range(ya_ref.shape[0])], axis=1)

    def store_out(cols, value):
        out_ref[:, cols] = value

    for _ in _post_body(lambda cols: x_ref[:, cols], lambda: hm_ref[...], get_ya, gmix_ref, gmh_ref,
                        wz_ref, wum_ref, wua_ref, wo_ref, store_out):
        pass


def _mixer_post_kernel(q_ref, k_ref, v_ref, gt_ref, bias_ref, c0_ref, n0_ref, m0_ref,
                       x_ref, ya_ref, gmix_ref, gmh_ref, wz_ref, wum_ref, wua_ref, wo_ref,
                       out_ref, c_ref, n_ref, m_ref, hm_ref, *, L, nchunk, nsteps):
    s = pl.program_id(0)
    cur = s % 2
    live = s < nsteps
    nb_step = q_ref.shape[0]
    rows = nb_step * L

    @pl.when(s == 0)
    def _():
        hm_ref[1] = jnp.zeros(hm_ref.shape[1:], F32)

    @pl.when(jnp.logical_and(live, s % nchunk == 0))
    def _():
        c_ref[...] = c0_ref[...]
        n_ref[...] = n0_ref[...]
        m_ref[...] = m0_ref[...]

    def store_h(nb, cols, value):
        hm_ref[cur, nb, :, cols] = value

    def store_out(cols, value):
        out_ref[:, :, cols] = value.reshape(nb_step, L, value.shape[1])

    get_x = lambda cols: x_ref[:, :, cols].reshape(rows, -1)
    get_hm = lambda: hm_ref[1 - cur].reshape(rows, M_WIDTH)
    get_ya = lambda: jnp.concatenate([ya_ref[p].reshape(rows, 128) for p in range(ya_ref.shape[0])], axis=1)
    mixer = _mlstm_body(q_ref, k_ref, v_ref, gt_ref, bias_ref, c_ref, n_ref, m_ref,
                        store_h, lambda write: pl.when(live)(write), L)
    post = _post_body(get_x, get_hm, get_ya, gmix_ref, gmh_ref, wz_ref, wum_ref, wua_ref, wo_ref, store_out)
    mixer_done = post_done = False
    while not (mixer_done and post_done):
        if not post_done:
            post_done = next(post, "done") == "done"
        if not mixer_done:
            mixer_done = next(mixer, "done") == "done"


def _mixer_post(q, k, v, gates, bias, c0, n0, m0, x, y_a, g_mix, g_mhead, w_z, w_up_m, w_up_a, w_out, L, nb_step):
    nb, seq, _ = q.shape
    nchunk = seq // L
    nsteps = (nb // nb_step) * nchunk
    cur = lambda s: (jnp.minimum(s, nsteps - 1) // nchunk, jnp.minimum(s, nsteps - 1) % nchunk)
    prv = lambda s: (jnp.maximum(s - 1, 0) // nchunk, jnp.maximum(s - 1, 0) % nchunk)
    tok = pl.BlockSpec((nb_step, L, M_WIDTH), lambda s: (*cur(s), 0))
    st_c = pl.BlockSpec((nb_step, M_HEADS, M_DV, M_DK), lambda s: (cur(s)[0], 0, 0, 0))
    st_n = pl.BlockSpec((nb_step, M_HEADS, M_DK), lambda s: (cur(s)[0], 0, 0))
    st_m = pl.BlockSpec((nb_step, M_HEADS, 1), lambda s: (cur(s)[0], 0, 0))
    x_spec = pl.BlockSpec((nb_step, L, D_MODEL), lambda s: (*prv(s), 0))
    return pl.pallas_call(
        functools.partial(_mixer_post_kernel, L=L, nchunk=nchunk, nsteps=nsteps),
        grid=(nsteps + 1,),
        in_specs=[tok, tok, tok, pl.BlockSpec((nb_step, 8, L), lambda s: (cur(s)[0], 0, cur(s)[1])),
                  _resident((8, 1)),
                  pl.BlockSpec(st_c.block_shape, st_c.index_map, pipeline_mode=pl.Buffered(1)), st_n, st_m,
                  x_spec, pl.BlockSpec((y_a.shape[0], nb_step, L, 128), lambda s: (0, *prv(s), 0)),
                  _resident((1, D_MODEL)), _resident((1, M_WIDTH)), _resident(w_z.shape),
                  _resident(w_up_m.shape), _resident(w_up_a.shape), _resident(w_out.shape)],
        out_specs=(x_spec, st_c, st_n, st_m),
        out_shape=(jax.ShapeDtypeStruct(x.shape, F32),
                   jax.ShapeDtypeStruct(c0.shape, F32), jax.ShapeDtypeStruct(n0.shape, F32),
                   jax.ShapeDtypeStruct(m0.shape, F32)),
        scratch_shapes=[pltpu.VMEM((2, nb_step, L, M_WIDTH), F32)],
        compiler_params=_params(1),
        name="mixer_post",
    )(q, k, v, gates, bias, c0, n0, m0, x, y_a, g_mix, g_mhead, w_z, w_up_m, w_up_a, w_out)


def _post(x, hm, y_a, g_mix, g_mhead, w_z, w_up_m, w_up_a, w_out, tm=1024):
    n = x.shape[0]
    tm = min(tm, n)
    row = lambda w: pl.BlockSpec((tm, w), lambda i: (i, 0))
    return pl.pallas_call(
        _post_kernel,
        grid=(n // tm,),
        in_specs=[row(D_MODEL), row(M_WIDTH), pl.BlockSpec((y_a.shape[0], tm, 128), lambda i: (0, i, 0)),
                  _resident((1, D_MODEL)), _resident((1, M_WIDTH)), _resident(w_z.shape),
                  _resident(w_up_m.shape), _resident(w_up_a.shape), _resident(w_out.shape)],
        out_specs=row(D_MODEL),
        out_shape=jax.ShapeDtypeStruct((n, D_MODEL), F32),
        compiler_params=_params(1),
        name="post",
    )(x, hm, y_a, g_mix, g_mhead, w_z, w_up_m, w_up_a, w_out)


def _rope_tables(pos):
    half = A_HD // 2
    inv_freq = ROPE_THETA ** (-2.0 * jnp.arange(half, dtype=F32) / A_HD)
    ang = pos.astype(F32)[:, None] * inv_freq[None, :]
    cos = jnp.cos(ang)
    sin = jnp.sin(ang)
    return jnp.tile(cos, (1, 4)), jnp.tile(jnp.concatenate([-sin, sin], axis=1), (1, 2))


def _window_rows(kvf, nbatch, rows):
    res = []
    for g, r in enumerate(rows):
        kk = kvf[:, kvf.shape[1] - r:, g * A_GW:(g + 1) * A_GW].reshape(nbatch, r, A_HPG, A_HD)
        vv = kvf[:, kvf.shape[1] - r:, A_WIDTH + g * A_GW:A_WIDTH + (g + 1) * A_GW].reshape(nbatch, r, A_HPG, A_HD)
        res.append(jnp.stack([kk, vv], axis=2))
    return res


def kernel(x_prompt, x_sample, state_mlstm_C, state_mlstm_n, state_mlstm_m, cache_win128_kv, cache_win512_kv, cache_win2048_kv, g_ffn1, w1_gate, w1_up, w1_down, g_mix, w_in, b_igate, b_fgate, g_mhead, w_up_m, w_up_a, w_out, g_ffn2, w2_gate, w2_up, w2_down, g_final):
    B, S, _ = x_prompt.shape
    BS, T, _ = x_sample.shape
    depth = w_in.shape[0]
    caches_all = (cache_win128_kv, cache_win512_kv, cache_win2048_kv)

    cos_p, sin_p = _rope_tables(jnp.arange(S, dtype=jnp.int32))
    cos_s, sin_s = _rope_tables(PAST_LEN + jnp.arange(T, dtype=jnp.int32))
    cos_s = jnp.tile(cos_s, (BS, 1))
    sin_s = jnp.tile(sin_s, (BS, 1))

    xp = x_prompt.reshape(B * S, D_MODEL)
    xs = x_sample.reshape(BS * T, D_MODEL)
    L_P = 256
    L_S = 128
    NB_P = 4
    FFN1_TM, FFN2_TM = 1024, 512
    prompt_rows = tuple(min(w, S) for w, _ in A_GROUPS)
    tail = max(prompt_rows)
    dils = tuple(d for _, d in A_GROUPS)
    acc = [[] for _ in range(12)]
    for l in range(depth):
        bf = lambda w: w[l].astype(BF16)
        vec = lambda g: g[l].reshape(1, -1)
        wi = w_in[l]
        w_qkv = jnp.concatenate([wi[:, _C_QM:_C_IG], wi[:, _C_QA:_C_GM]], axis=1).astype(BF16)
        w_gate_t = wi[:, _C_IG:_C_OG].T.astype(BF16)
        w_z = jnp.concatenate([wi[:, _C_OG:_C_QA], wi[:, _C_GM:_C_END]], axis=1).astype(BF16)
        ffn1 = (vec(g_ffn1), bf(w1_gate), bf(w1_up), bf(w1_down))
        ffn2 = (vec(g_ffn2), bf(w2_gate), bf(w2_up), bf(w2_down))
        post_w = (vec(g_mix), vec(g_mhead), w_z, bf(w_up_m), bf(w_up_a), bf(w_out))
        gate_bias = jnp.concatenate([b_igate[l], b_fgate[l]]).reshape(8, 1)
        last = l == depth - 1
        gfin = g_final.reshape(1, -1) if last else None

        xs = _ffn(xs, *ffn1)
        qm, km, vm, *qkv_s, kvf_s, gt = _inproj(xs, vec(g_mix), w_qkv, w_gate_t, cos_s, sin_s, 1, BS * T, BS * T,
                                                (1,) * len(A_GROUPS))
        kvf_s = kvf_s.reshape(BS, T, 2 * A_WIDTH)
        pad_tok = lambda t: jnp.pad(t.reshape(BS, T, M_WIDTH), ((0, 0), (0, L_S - T), (0, 0)))
        gt = gt.reshape(8, BS, T).transpose(1, 0, 2)
        gt = jnp.concatenate([jnp.pad(gt[:, 0:4], ((0, 0), (0, 0), (0, L_S - T)), constant_values=-1e30),
                              jnp.pad(gt[:, 4:8], ((0, 0), (0, 0), (0, L_S - T)), constant_values=1e30)], axis=1)
        mlstm_s = _mlstm_rider(pad_tok(qm), pad_tok(km), pad_tok(vm), gt, gate_bias, state_mlstm_C[l],
                               state_mlstm_n[l], state_mlstm_m[l].reshape(BS, M_HEADS, 1), B * S // FFN1_TM)
        caches = [jnp.transpose(c[l], (0, 2, 3, 4, 1)).reshape(BS, 2 * A_GW, c.shape[2]) for c in caches_all]
        attn_s = _attn_cached_rider([q.astype(F32).reshape(BS, T, A_GW) for q in qkv_s[0:3]], kvf_s, caches,
                                    B * S // FFN2_TM)

        xp, (hm_s, c_s, n_s, m_s) = _ffn(xp, *ffn1, tm=FFN1_TM, rider=mlstm_s)
        qm, km, vm, *qkv_a, kvf, gt = _inproj(xp, vec(g_mix), w_qkv, w_gate_t, cos_p, sin_p, B, S, tail, dils)
        y_a = _attn_prompt(qkv_a[0:3], qkv_a[3:6], qkv_a[6:9], B, S)
        xp, c_p, n_p, m_p = _mixer_post(
            qm.reshape(B, S, M_WIDTH), km.reshape(B, S, M_WIDTH), vm.reshape(B, S, M_WIDTH), gt, gate_bias,
            jnp.zeros((B, M_HEADS, M_DV, M_DK), F32), jnp.zeros((B, M_HEADS, M_DK), F32),
            jnp.zeros((B, M_HEADS, 1), F32), xp.reshape(B, S, D_MODEL), y_a.reshape(y_a.shape[0], B, S, 128),
            *post_w, L_P, NB_P)
        xp = xp.reshape(B * S, D_MODEL)
        xp, (ya_s,) = _ffn(xp, *ffn2, g_final=gfin, tm=FFN2_TM, rider=attn_s)
        kv_p = _window_rows(kvf, B, prompt_rows)
        for i, t in enumerate((c_p, n_p, m_p.reshape(B, M_HEADS), *kv_p)):
            acc[i].append(t)

        hm_s = hm_s[:, :T].reshape(BS * T, M_WIDTH)
        xs = _post(xs, hm_s, ya_s.reshape(BS * T, A_GW // 128, 128).transpose(1, 0, 2), *post_w)
        xs = _ffn(xs, *ffn2, g_final=gfin)
        kv_s = _window_rows(kvf_s, BS, (T,) * len(A_GROUPS))
        for i, t in enumerate((c_s, n_s, m_s.reshape(BS, M_HEADS), *kv_s)):
            acc[6 + i].append(t)

    stacked = [jnp.stack(a) for a in acc]
    return (xp.reshape(B, S, D_MODEL), xs.reshape(BS, T, D_MODEL), *stacked)
```
